```python
import math
import jax, jax.numpy as jnp
from jax import lax
import numpy as np

D_MODEL = 2048
BATCH = 4
SEQ = 2048
DEPTH = 1

DIFF_V_DIM = 128
DIFF_QK_DIM = 64
DIFF_WIDTH = D_MODEL // 2
DIFF_HEADS = DIFF_WIDTH // DIFF_V_DIM
FOX_HEAD_DIM = 128
FOX_WIDTH = D_MODEL // 2
FOX_HEADS = FOX_WIDTH // FOX_HEAD_DIM
ROPE_THETA = 500000.0
ROPE_DIM = DIFF_QK_DIM // 4
BLOCK_Q = 128
EPS = 1e-6

SPLIT_SIZES = (
    DIFF_HEADS * 2 * DIFF_QK_DIM,
    DIFF_HEADS * 2 * DIFF_QK_DIM,
    DIFF_WIDTH,
    DIFF_WIDTH,
    FOX_WIDTH,
    FOX_WIDTH,
    FOX_WIDTH,
    FOX_WIDTH,
    FOX_HEADS,
    D_MODEL,
    D_MODEL,
)
N_IN = sum(SPLIT_SIZES)

kernel_name = "hybrid_diffattn_fox_gated_block"


def rmsnorm(x, g):
    xf = x.astype(jnp.float32)
    y = xf * lax.rsqrt(jnp.mean(xf * xf, axis=-1, keepdims=True) + EPS)
    return (y * g.astype(jnp.float32)).astype(x.dtype)


def partial_rope(x, pos):
    half = ROPE_DIM // 2
    inv = ROPE_THETA ** (-jnp.arange(half, dtype=jnp.float32) * 2.0 / ROPE_DIM)
    ang = pos.astype(jnp.float32)[:, None] * inv[None, :]
    shape = (1, pos.shape[0]) + (1,) * (x.ndim - 3) + (half,)
    cos = jnp.cos(ang).reshape(shape).astype(x.dtype)
    sin = jnp.sin(ang).reshape(shape).astype(x.dtype)
    x1 = x[..., :half]
    x2 = x[..., half:ROPE_DIM]
    return jnp.concatenate([x1 * cos - x2 * sin, x2 * cos + x1 * sin, x[..., ROPE_DIM:]], axis=-1)


def causal_mask(q0, kend):
    qpos = jnp.arange(q0, q0 + BLOCK_Q)
    kpos = jnp.arange(kend)
    return qpos[:, None] >= kpos[None, :]


def diff_attention(q, k, v, lam):
    S = q.shape[1]
    scale = DIFF_QK_DIM ** -0.5
    outs = []
    for i in range(S // BLOCK_Q):
        q0, q1 = i * BLOCK_Q, (i + 1) * BLOCK_Q
        s = jnp.einsum('bqhmd,bkhmd->bhmqk', q[:, q0:q1], k[:, :q1]).astype(jnp.float32) * scale
        s = jnp.where(causal_mask(q0, q1), s, -jnp.inf)
        p = jax.nn.softmax(s, axis=-1)
        a = p[:, :, 0] - lam * p[:, :, 1]
        outs.append(jnp.einsum('bhqk,bkhd->bqhd', a.astype(v.dtype), v[:, :q1]))
    return jnp.concatenate(outs, axis=1)


def forgetting_attention(q, k, v, log_f):
    S = q.shape[1]
    scale = FOX_HEAD_DIM ** -0.5
    c = jnp.cumsum(log_f, axis=1).transpose(0, 2, 1)
    outs = []
    for i in range(S // BLOCK_Q):
        q0, q1 = i * BLOCK_Q, (i + 1) * BLOCK_Q
        s = jnp.einsum('bqhd,bkhd->bhqk', q[:, q0:q1], k[:, :q1]).astype(jnp.float32) * scale
        s = s + (c[:, :, q0:q1, None] - c[:, :, None, :q1])
        s = jnp.where(causal_mask(q0, q1), s, -jnp.inf)
        p = jax.nn.softmax(s, axis=-1)
        outs.append(jnp.einsum('bhqk,bkhd->bqhd', p.astype(v.dtype), v[:, :q1]))
    return jnp.concatenate(outs, axis=1)


def hybrid_mixer(h, w_in, lq1, lk1, lq2, lk2, g_subln, b_f, w_proj_a, w_proj_b, w_out, layer_idx):
    B, S, _ = h.shape
    pos = jnp.arange(S)
    z = jnp.einsum('bsd,dn->bsn', h, w_in)
    idx = np.cumsum(np.array(SPLIT_SIZES))[:-1].tolist()
    qa, ka, va, ga, qb, kb, vb, gb, fl, ma, mb = jnp.split(z, idx, axis=-1)

    qa = partial_rope(qa.reshape(B, S, DIFF_HEADS, 2, DIFF_QK_DIM), pos)
    ka = partial_rope(ka.reshape(B, S, DIFF_HEADS, 2, DIFF_QK_DIM), pos)
    va = va.reshape(B, S, DIFF_HEADS, DIFF_V_DIM)
    lam_init = 0.8 - 0.6 * math.exp(-0.3 * layer_idx)
    lam = (jnp.exp(jnp.sum(lq1.astype(jnp.float32) * lk1.astype(jnp.float32)))
           - jnp.exp(jnp.sum(lq2.astype(jnp.float32) * lk2.astype(jnp.float32))) + lam_init)
    oa = diff_attention(qa, ka, va, lam)
    oa = rmsnorm(oa, g_subln) * (1.0 - lam_init)
    oa = oa.reshape(B, S, DIFF_WIDTH) * jax.nn.silu(ga)
    ya = jnp.einsum('bsw,wd->bsd', oa, w_proj_a)

    log_f = jax.nn.log_sigmoid((fl + b_f).astype(jnp.float32))
    ob = forgetting_attention(qb.reshape(B, S, FOX_HEADS, FOX_HEAD_DIM),
                              kb.reshape(B, S, FOX_HEADS, FOX_HEAD_DIM),
                              vb.reshape(B, S, FOX_HEADS, FOX_HEAD_DIM), log_f)
    ob = ob.reshape(B, S, FOX_WIDTH) * jax.nn.silu(gb)
    yb = jnp.einsum('bsw,wd->bsd', ob, w_proj_b)

    m = jax.nn.sigmoid(ma) * ya + jax.nn.sigmoid(mb) * yb
    return jnp.einsum('bsd,de->bse', m, w_out)


def setup_inputs(seed: int = 0) -> dict:
    key = jax.random.key(seed)
    ks = jax.random.split(key, 14)
    f32 = jnp.float32
    x = jax.random.normal(ks[0], (BATCH, SEQ, D_MODEL), f32)
    g_norm = 1.0 + 0.02 * jax.random.normal(ks[1], (DEPTH, D_MODEL), f32)
    w_in = jax.random.normal(ks[2], (DEPTH, D_MODEL, N_IN), f32) * D_MODEL ** -0.5
    lambda_q1 = 0.1 * jax.random.normal(ks[3], (DEPTH, DIFF_QK_DIM), f32)
    lambda_k1 = 0.1 * jax.random.normal(ks[4], (DEPTH, DIFF_QK_DIM), f32)
    lambda_q2 = 0.1 * jax.random.normal(ks[5], (DEPTH, DIFF_QK_DIM), f32)
    lambda_k2 = 0.1 * jax.random.normal(ks[6], (DEPTH, DIFF_QK_DIM), f32)
    g_subln = 1.0 + 0.02 * jax.random.normal(ks[7], (DEPTH, DIFF_V_DIM), f32)
    b_forget = jax.random.uniform(ks[8], (DEPTH, FOX_HEADS), f32, minval=1.0, maxval=4.0)
    w_proj_a = jax.random.normal(ks[9], (DEPTH, DIFF_WIDTH, D_MODEL), f32) * DIFF_WIDTH ** -0.5
    w_proj_b = jax.random.normal(ks[10], (DEPTH, FOX_WIDTH, D_MODEL), f32) * FOX_WIDTH ** -0.5
    w_out = jax.random.normal(ks[11], (DEPTH, D_MODEL, D_MODEL), f32) * D_MODEL ** -0.5
    g_final = 1.0 + 0.02 * jax.random.normal(ks[12], (D_MODEL,), f32)
    return {"x": x, "g_norm": g_norm, "w_in": w_in,
            "lambda_q1": lambda_q1, "lambda_k1": lambda_k1,
            "lambda_q2": lambda_q2, "lambda_k2": lambda_k2,
            "g_subln": g_subln, "b_forget": b_forget,
            "w_proj_a": w_proj_a, "w_proj_b": w_proj_b, "w_out": w_out,
            "g_final": g_final}


def reference(x, g_norm, w_in, lambda_q1, lambda_k1, lambda_q2, lambda_k2, g_subln,
              b_forget, w_proj_a, w_proj_b, w_out, g_final):
    for l in range(DEPTH):
        h = rmsnorm(x, g_norm[l])
        x = x + hybrid_mixer(h, w_in[l], lambda_q1[l], lambda_k1[l], lambda_q2[l], lambda_k2[l],
                             g_subln[l], b_forget[l], w_proj_a[l], w_proj_b[l], w_out[l], l)
    return rmsnorm(x, g_final)
```

```python
import functools
import math

import jax
import jax.numpy as jnp
from jax import lax
from jax.experimental import pallas as pl
from jax.experimental.pallas import tpu as pltpu

F32 = jnp.float32
BF16 = jnp.bfloat16

LANES = 128
DIFF_QK_DIM = 64
DIFF_V_DIM = 128
FOX_HEAD_DIM = 128
ROPE_DIM = 16
ROPE_THETA = 500000.0
EPS = 1e-6
LAMBDA_INIT = 0.8 - 0.6 * math.exp(-0.3 * 0)

MIB = 1024 * 1024


def _rope_tables(seq):
    half = ROPE_DIM // 2
    inv = ROPE_THETA ** (-jnp.arange(half, dtype=F32) * 2.0 / ROPE_DIM)
    ang = jnp.arange(seq).astype(F32)[:, None] * inv[None, :]
    cos, sin = jnp.cos(ang), jnp.sin(ang)
    ones = jnp.ones((seq, DIFF_QK_DIM - ROPE_DIM), F32)
    zeros = jnp.zeros((seq, DIFF_QK_DIM - ROPE_DIM), F32)
    zh = jnp.zeros((seq, half), F32)
    c_mul = jnp.concatenate([cos, cos, ones], axis=1)
    up_mul = jnp.concatenate([zh, sin, zeros], axis=1)
    dn_mul = jnp.concatenate([-sin, zh, zeros], axis=1)
    reps = LANES // DIFF_QK_DIM
    return jnp.stack([jnp.tile(t, (1, reps)) for t in (c_mul, up_mul, dn_mul)])


def _inproj_kernel(x_ref, g_ref, w_ref, wf_ref, rope_ref, z_ref, fl_ref, h_ref, *,
                   q_tiles, k_tiles, q_scale):
    j = pl.program_id(1)

    @pl.when(j == 0)
    def _():
        xf = x_ref[...]
        ms = jnp.mean(xf * xf, axis=-1, keepdims=True)
        hb = ((xf * lax.rsqrt(ms + EPS)) * g_ref[...]).astype(BF16)
        h_ref[...] = hb
        fl_ref[...] = jnp.dot(hb, wf_ref[...], preferred_element_type=F32)

    acc = jnp.dot(h_ref[...], w_ref[...], preferred_element_type=F32)
    tn = acc.shape[1]

    def rope_store(scale):
        half = ROPE_DIM // 2
        c_mul, up_mul, dn_mul = rope_ref[0], rope_ref[1], rope_ref[2]
        for c in range(tn // LANES):
            a = acc[:, c * LANES:(c + 1) * LANES]
            r = (a * c_mul + pltpu.roll(a, half, 1) * up_mul
                 + pltpu.roll(a, LANES - half, 1) * dn_mul)
            if scale != 1.0:
                r = r * scale
            z_ref[:, c * LANES:(c + 1) * LANES] = r.astype(z_ref.dtype)

    @pl.when(j < q_tiles)
    def _():
        rope_store(q_scale)

    @pl.when((j >= q_tiles) & (j < q_tiles + k_tiles))
    def _():
        rope_store(1.0)

    @pl.when(j >= q_tiles + k_tiles)
    def _():
        z_ref[...] = acc.astype(z_ref.dtype)


def _inproj(x2, g_norm, w_cat, w_f, rope, *, seq, tm, tn, qk_cols):
    m, d = x2.shape
    n = w_cat.shape[1]
    q_tiles = k_tiles = qk_cols // tn
    vmem = (2 * tm * d * 4 + tm * d * 2 + 2 * d * tn * 2 + 2 * tm * tn * 2
            + 2 * 3 * tm * LANES * 4 + 2 * tm * LANES * 4 + 2 * d * LANES * 2
            + 3 * tm * tn * 4)
    return pl.pallas_call(
        functools.partial(_inproj_kernel, q_tiles=q_tiles, k_tiles=k_tiles,
                          q_scale=DIFF_QK_DIM ** -0.5),
        grid=(m // tm, n // tn),
        in_specs=[
            pl.BlockSpec((tm, d), lambda i, j: (i, 0)),
            pl.BlockSpec((1, d), lambda i, j: (0, 0)),
            pl.BlockSpec((d, tn), lambda i, j: (0, j)),
            pl.BlockSpec((d, LANES), lambda i, j: (0, 0)),
            pl.BlockSpec((3, tm, LANES), lambda i, j: (0, i % (seq // tm), 0)),
        ],
        out_specs=[
            pl.BlockSpec((tm, tn), lambda i, j: (i, j)),
            pl.BlockSpec((tm, LANES), lambda i, j: (i, 0)),
        ],
        out_shape=[jax.ShapeDtypeStruct((m, n), BF16),
                   jax.ShapeDtypeStruct((m, LANES), F32)],
        scratch_shapes=[pltpu.VMEM((tm, d), BF16)],
        compiler_params=pltpu.CompilerParams(
            dimension_semantics=("arbitrary", "arbitrary"),
            vmem_limit_bytes=vmem + 4 * MIB),
        name="inproj",
    )(x2, g_norm, w_cat, w_f, rope)


def _forget_cumsum_kernel(fl_ref, b_ref, c_ref, *, heads):
    t = fl_ref[...] + b_ref[...]
    lf = jnp.minimum(t, 0.0) - jnp.log1p(jnp.exp(-jnp.abs(t)))
    seq = lf.shape[0]
    row = lax.broadcasted_iota(jnp.int32, lf.shape, 0)
    d = 1
    while d < seq:
        lf = lf + jnp.where(row >= d, pltpu.roll(lf, d, 0), 0.0)
        d *= 2
    c_ref[0] = lf.T[:heads, :]


def _forget_cumsum(fl, b_pad, *, batch, seq, heads):
    return pl.pallas_call(
        functools.partial(_forget_cumsum_kernel, heads=heads),
        grid=(batch,),
        in_specs=[pl.BlockSpec((seq, LANES), lambda b: (b, 0)),
                  pl.BlockSpec((1, LANES), lambda b: (0, 0))],
        out_specs=pl.BlockSpec((1, heads, seq), lambda b: (b, 0, 0)),
        out_shape=jax.ShapeDtypeStruct((batch, heads, seq), F32),
        compiler_params=pltpu.CompilerParams(dimension_semantics=("arbitrary",)),
        name="forget_cumsum",
    )(fl, b_pad)


def _lane_group_max(s):
    groups = [s[:, c * LANES:(c + 1) * LANES] for c in range(s.shape[1] // LANES)]
    out = groups[0]
    for g in groups[1:]:
        out = jnp.maximum(out, g)
    return out


def _silu(x):
    return x * (1.0 / (1.0 + jnp.exp(-x)))


def _diff_attn_kernel(q_ref, k_ref, v_ref, g_ref, lq1_ref, lk1_ref, lq2_ref, lk2_ref,
                      gsub_ref, o_ref, s_scr, *, tq):
    seq = q_ref.shape[0]
    rows = 2 * tq
    lam = (jnp.exp(jnp.sum(lq1_ref[...] * lk1_ref[...], axis=-1, keepdims=True))
           - jnp.exp(jnp.sum(lq2_ref[...] * lk2_ref[...], axis=-1, keepdims=True))
           + LAMBDA_INIT)
    lane = lax.broadcasted_iota(jnp.int32, (tq, LANES), 1)
    qpos = lax.broadcasted_iota(jnp.int32, (rows, tq), 0) & (tq - 1)
    kpos = lax.broadcasted_iota(jnp.int32, (rows, tq), 1)
    causal = kpos <= qpos
    for i in range(seq // tq):
        q = q_ref[i * tq:(i + 1) * tq, :]
        zero = jnp.zeros_like(q)
        qq = jnp.concatenate([jnp.where(lane < DIFF_QK_DIM, q, zero),
                              jnp.where(lane >= DIFF_QK_DIM, q, zero)], axis=0)
        mrun = None
        for j in range(i + 1):
            kj = k_ref[j * tq:(j + 1) * tq, :]
            s = lax.dot_general(qq, kj, (((1,), (1,)), ((), ())), preferred_element_type=F32)
            if j == i:
                s = jnp.where(causal, s, -jnp.inf)
            s_scr[:, j * tq:(j + 1) * tq] = s
            bm = _lane_group_max(s)
            mrun = bm if mrun is None else jnp.maximum(mrun, bm)
        m = jnp.max(mrun, axis=-1, keepdims=True)
        kv = (i + 1) * tq
        p = jnp.exp(s_scr[:, :kv] - m)
        l = jnp.sum(p, axis=-1, keepdims=True)
        acc = jnp.dot(p.astype(BF16), v_ref[:kv, :], preferred_element_type=F32)
        o_all = acc * (1.0 / l)
        o = o_all[:tq] - lam * o_all[tq:]
        ms = jnp.mean(o * o, axis=-1, keepdims=True)
        on = (o * lax.rsqrt(ms + EPS)) * gsub_ref[...] * (1.0 - LAMBDA_INIT)
        gate = g_ref[i * tq:(i + 1) * tq, :].astype(F32)
        o_ref[i * tq:(i + 1) * tq, :] = (on * _silu(gate)).astype(o_ref.dtype)


def _fox_attn_kernel(q_ref, k_ref, v_ref, g_ref, c_ref, o_ref, s_scr, *, tq, scale):
    seq = q_ref.shape[0]
    h = pl.program_id(1)
    crow = c_ref[0, pl.ds(h, 1), :]
    ccol = jnp.broadcast_to(crow, (LANES, seq)).T
    qpos = lax.broadcasted_iota(jnp.int32, (tq, tq), 0)
    kpos = lax.broadcasted_iota(jnp.int32, (tq, tq), 1)
    causal = kpos <= qpos
    for i in range(seq // tq):
        q = q_ref[i * tq:(i + 1) * tq, :]
        cq = ccol[i * tq:(i + 1) * tq, :]
        mrun = None
        for j in range(i + 1):
            kj = k_ref[j * tq:(j + 1) * tq, :]
            s = lax.dot_general(q, kj, (((1,), (1,)), ((), ())), preferred_element_type=F32)
            parts = []
            for c in range(tq // LANES):
                ck = crow[:, j * tq + c * LANES:j * tq + (c + 1) * LANES]
                parts.append(s[:, c * LANES:(c + 1) * LANES] * scale + (cq - ck))
            s = jnp.concatenate(parts, axis=1)
            if j == i:
                s = jnp.where(causal, s, -jnp.inf)
            s_scr[:, j * tq:(j + 1) * tq] = s
            bm = _lane_group_max(s)
            mrun = bm if mrun is None else jnp.maximum(mrun, bm)
        m = jnp.max(mrun, axis=-1, keepdims=True)
        kv = (i + 1) * tq
        p = jnp.exp(s_scr[:, :kv] - m)
        l = jnp.sum(p, axis=-1, keepdims=True)
        acc = jnp.dot(p.astype(BF16), v_ref[:kv, :], preferred_element_type=F32)
        gate = g_ref[i * tq:(i + 1) * tq, :].astype(F32)
        o_ref[i * tq:(i + 1) * tq, :] = (acc * (1.0 / l) * _silu(gate)).astype(o_ref.dtype)


def _head_spec(seq, col0):
    return pl.BlockSpec((seq, LANES), lambda b, h: (b, col0 // LANES + h))


def _diff_attn(z, lq1, lk1, lq2, lk2, g_subln, *, batch, seq, heads, col0, tq):
    width = heads * DIFF_V_DIM
    vec = pl.BlockSpec((1, DIFF_QK_DIM), lambda b, h: (0, 0))
    return pl.pallas_call(
        functools.partial(_diff_attn_kernel, tq=tq),
        grid=(batch, heads),
        in_specs=[_head_spec(seq, col0), _head_spec(seq, col0 + width),
                  _head_spec(seq, col0 + 2 * width), _head_spec(seq, col0 + 3 * width),
                  vec, vec, vec, vec,
                  pl.BlockSpec((1, DIFF_V_DIM), lambda b, h: (0, 0))],
        out_specs=pl.BlockSpec((seq, LANES), lambda b, h: (b, h)),
        out_shape=jax.ShapeDtypeStruct((batch * seq, width), BF16),
        scratch_shapes=[pltpu.VMEM((2 * tq, seq), F32)],
        compiler_params=pltpu.CompilerParams(
            dimension_semantics=("arbitrary", "arbitrary"), vmem_limit_bytes=48 * MIB),
        name="diff_attn",
    )(z, z, z, z, lq1, lk1, lq2, lk2, g_subln)


def _fox_attn(z, c, *, batch, seq, heads, col0, tq):
    width = heads * FOX_HEAD_DIM
    return pl.pallas_call(
        functools.partial(_fox_attn_kernel, tq=tq, scale=FOX_HEAD_DIM ** -0.5),
        grid=(batch, heads),
        in_specs=[_head_spec(seq, col0), _head_spec(seq, col0 + width),
                  _head_spec(seq, col0 + 2 * width), _head_spec(seq, col0 + 3 * width),
                  pl.BlockSpec((1, heads, seq), lambda b, h: (b, 0, 0))],
        out_specs=pl.BlockSpec((seq, LANES), lambda b, h: (b, h)),
        out_shape=jax.ShapeDtypeStruct((batch * seq, width), BF16),
        scratch_shapes=[pltpu.VMEM((tq, seq), F32)],
        compiler_params=pltpu.CompilerParams(
            dimension_semantics=("arbitrary", "arbitrary"), vmem_limit_bytes=48 * MIB),
        name="fox_attn",
    )(z, z, z, z, c)


def _merge_kernel(oa_ref, ob_ref, wa_ref, wb_ref, ma_ref, mb_ref, m_ref):
    ya = jnp.dot(oa_ref[...], wa_ref[...], preferred_element_type=F32)
    yb = jnp.dot(ob_ref[...], wb_ref[...], preferred_element_type=F32)
    ga = jax.nn.sigmoid(ma_ref[...].astype(F32))
    gb = jax.nn.sigmoid(mb_ref[...].astype(F32))
    m_ref[...] = (ga * ya + gb * yb).astype(m_ref.dtype)


def _merge(oa, ob, wa, wb, z, *, ma_col0, mb_col0, tm, tn):
    m, wdt = oa.shape
    d = wa.shape[1]
    return pl.pallas_call(
        _merge_kernel,
        grid=(m // tm, d // tn),
        in_specs=[pl.BlockSpec((tm, wdt), lambda i, j: (i, 0)),
                  pl.BlockSpec((tm, wdt), lambda i, j: (i, 0)),
                  pl.BlockSpec((wdt, tn), lambda i, j: (0, j)),
                  pl.BlockSpec((wdt, tn), lambda i, j: (0, j)),
                  pl.BlockSpec((tm, tn), lambda i, j: (i, ma_col0 // tn + j)),
                  pl.BlockSpec((tm, tn), lambda i, j: (i, mb_col0 // tn + j))],
        out_specs=pl.BlockSpec((tm, tn), lambda i, j: (i, j)),
        out_shape=jax.ShapeDtypeStruct((m, d), BF16),
        compiler_params=pltpu.CompilerParams(
            dimension_semantics=("arbitrary", "arbitrary"), vmem_limit_bytes=40 * MIB),
        name="merge",
    )(oa, ob, wa, wb, z, z)


def _outproj_kernel(m_ref, w_ref, x_ref, g_ref, o_ref):
    y = x_ref[...] + jnp.dot(m_ref[...], w_ref[...], preferred_element_type=F32)
    ms = jnp.mean(y * y, axis=-1, keepdims=True)
    o_ref[...] = (y * lax.rsqrt(ms + EPS)) * g_ref[...]


def _outproj(mrg, w_out, x2, g_final, *, tm):
    m, d = x2.shape
    return pl.pallas_call(
        _outproj_kernel,
        grid=(m // tm,),
        in_specs=[pl.BlockSpec((tm, d), lambda i: (i, 0)),
                  pl.BlockSpec((d, d), lambda i: (0, 0)),
                  pl.BlockSpec((tm, d), lambda i: (i, 0)),
                  pl.BlockSpec((1, d), lambda i: (0, 0))],
        out_specs=pl.BlockSpec((tm, d), lambda i: (i, 0)),
        out_shape=jax.ShapeDtypeStruct((m, d), F32),
        compiler_params=pltpu.CompilerParams(
            dimension_semantics=("arbitrary",), vmem_limit_bytes=48 * MIB),
        name="outproj",
    )(mrg, w_out, x2, g_final)


def kernel(x, g_norm, w_in, lambda_q1, lambda_k1, lambda_q2, lambda_k2, g_subln, b_forget,
           w_proj_a, w_proj_b, w_out, g_final):
    batch, seq, d = x.shape
    diff_width = w_proj_a.shape[1]
    fox_width = w_proj_b.shape[1]
    diff_heads = diff_width // DIFF_V_DIM
    fox_heads = fox_width // FOX_HEAD_DIM
    main_cols = 4 * diff_width + 4 * fox_width
    assert w_in.shape[0] == 1 and w_in.shape[2] == main_cols + fox_heads + 2 * d

    w = w_in[0]
    w_cat = jnp.concatenate([w[:, :main_cols], w[:, main_cols + fox_heads:]], axis=1).astype(BF16)
    w_f = jnp.pad(w[:, main_cols:main_cols + fox_heads],
                  ((0, 0), (0, LANES - fox_heads))).astype(BF16)
    b_pad = jnp.pad(b_forget, ((0, 0), (0, LANES - fox_heads)))
    wa = w_proj_a[0].astype(BF16)
    wb = w_proj_b[0].astype(BF16)
    wo = w_out[0].astype(BF16)

    x2 = x.reshape(batch * seq, d)
    z, fl = _inproj(x2, g_norm, w_cat, w_f, _rope_tables(seq),
                    seq=seq, tm=1024, tn=512, qk_cols=diff_width)
    c = _forget_cumsum(fl, b_pad, batch=batch, seq=seq, heads=fox_heads)
    oa = _diff_attn(z, lambda_q1, lambda_k1, lambda_q2, lambda_k2, g_subln,
                    batch=batch, seq=seq, heads=diff_heads, col0=0, tq=256)
    ob = _fox_attn(z, c, batch=batch, seq=seq, heads=fox_heads, col0=4 * diff_width, tq=256)
    mrg = _merge(oa, ob, wa, wb, z, ma_col0=main_cols, mb_col0=main_cols + d, tm=512, tn=512)
    out = _outproj(mrg, wo, x2, g_final.reshape(1, d), tm=256)
    return out.reshape(batch, seq, d)
```

```python
import functools
import math

import jax
import jax.numpy as jnp
from jax import lax
from jax.experimental import pallas as pl
from jax.experimental.pallas import tpu as pltpu

F32 = jnp.float32
BF16 = jnp.bfloat16

LANES = 128
DIFF_QK_DIM = 64
DIFF_V_DIM = 128
FOX_HEAD_DIM = 128
ROPE_DIM = 16
ROPE_THETA = 500000.0
EPS = 1e-6
LAMBDA_INIT = 0.8 - 0.6 * math.exp(-0.3 * 0)

MIB = 1024 * 1024


def _rope_tables(seq):
    half = ROPE_DIM // 2
    inv = ROPE_THETA ** (-jnp.arange(half, dtype=F32) * 2.0 / ROPE_DIM)
    ang = jnp.arange(seq).astype(F32)[:, None] * inv[None, :]
    cos, sin = jnp.cos(ang), jnp.sin(ang)
    rest = LANES // 2 - ROPE_DIM
    ones = jnp.ones((seq, rest), F32)
    zeros = jnp.zeros((seq, rest), F32)
    c_mul = jnp.concatenate([cos, cos, ones, cos, cos, ones], axis=1)
    s_mul = jnp.concatenate([-sin, -sin, zeros, sin, sin, zeros], axis=1)
    q_scale = DIFF_QK_DIM ** -0.5
    return jnp.stack([c_mul * q_scale, s_mul * q_scale, c_mul, s_mul])


def _map0_lanes(lane):
    half = ROPE_DIM // 2
    return (lane < half) | ((lane >= ROPE_DIM) & (lane < LANES // 2 + half))


def _pack_w_in_kernel(a_ref, b_ref, o_ref, wf_ref, *, perm_tiles, first_shifted, shift):
    j = pl.program_id(0)
    rows, tn = a_ref.shape
    half = ROPE_DIM // 2
    lane = lax.broadcasted_iota(jnp.int32, (rows, LANES), 1)

    @pl.when(j < perm_tiles)
    def _():
        lo = (lane >= half) & (lane < ROPE_DIM)
        hi = (lane >= LANES // 2) & (lane < LANES // 2 + half)
        dist = LANES // 2 - half
        for g in range(tn // LANES):
            a = a_ref[:, g * LANES:(g + 1) * LANES]
            o = jnp.where(lo, pltpu.roll(a, LANES - dist, 1),
                          jnp.where(hi, pltpu.roll(a, dist, 1), a))
            o_ref[:, g * LANES:(g + 1) * LANES] = o.astype(o_ref.dtype)

    @pl.when((j >= perm_tiles) & (j < first_shifted))
    def _():
        o_ref[...] = a_ref[...].astype(o_ref.dtype)

    @pl.when(j >= first_shifted)
    def _():
        for g in range(tn // LANES):
            cur = a_ref[:, g * LANES:(g + 1) * LANES]
            nxt = a_ref[:, (g + 1) * LANES:(g + 2) * LANES] if g + 1 < tn // LANES else b_ref[...]
            o = jnp.where(lane < LANES - shift, pltpu.roll(cur, LANES - shift, 1),
                          pltpu.roll(nxt, LANES - shift, 1))
            o_ref[:, g * LANES:(g + 1) * LANES] = o.astype(o_ref.dtype)

    @pl.when(j == first_shifted)
    def _():
        wf_ref[...] = a_ref[:, :LANES].astype(wf_ref.dtype)


def _pack_w_in(w, *, qk_cols, main_cols, shift, out_cols, tn):
    d = w.shape[0]
    per = tn // LANES
    first_shifted = main_cols // tn
    return pl.pallas_call(
        functools.partial(_pack_w_in_kernel, perm_tiles=2 * qk_cols // tn,
                          first_shifted=first_shifted, shift=shift),
        grid=(out_cols // tn,),
        in_specs=[pl.BlockSpec((d, tn), lambda j: (0, j)),
                  pl.BlockSpec((d, LANES),
                               lambda j: (0, per * (jnp.maximum(j, first_shifted) + 1)))],
        out_specs=[pl.BlockSpec((d, tn), lambda j: (0, j)),
                   pl.BlockSpec((d, LANES), lambda j: (0, 0))],
        out_shape=[jax.ShapeDtypeStruct((d, out_cols), BF16),
                   jax.ShapeDtypeStruct((d, LANES), BF16)],
        compiler_params=pltpu.CompilerParams(dimension_semantics=("arbitrary",),
                                             vmem_limit_bytes=32 * MIB),
        name="pack_w_in",
    )(w, w)


def _cast3_kernel(a_ref, b_ref, c_ref, oa_ref, ob_ref, oc_ref):
    oa_ref[...] = a_ref[...].astype(oa_ref.dtype)
    ob_ref[...] = b_ref[...].astype(ob_ref.dtype)
    oc_ref[...] = c_ref[...].astype(oc_ref.dtype)


def _cast3_bf16(a, b, c, *, steps):
    def spec(w):
        return pl.BlockSpec((w.shape[0] // steps, w.shape[1]), lambda i: (i, 0))
    return pl.pallas_call(
        _cast3_kernel,
        grid=(steps,),
        in_specs=[spec(a), spec(b), spec(c)],
        out_specs=[spec(a), spec(b), spec(c)],
        out_shape=[jax.ShapeDtypeStruct(w.shape, BF16) for w in (a, b, c)],
        compiler_params=pltpu.CompilerParams(dimension_semantics=("arbitrary",),
                                             vmem_limit_bytes=32 * MIB),
        name="cast_weights",
    )(a, b, c)


def _inproj_kernel(x_ref, g_ref, w_ref, wf_ref, rope_ref, z_ref, fl_ref, h_ref, *, chunk):
    j = pl.program_id(1)
    tn = w_ref.shape[1]

    def rope_tile(c_mul, s_mul):
        for c in range(tn // chunk):
            acc = jnp.dot(h_ref[...], w_ref[:, c * chunk:(c + 1) * chunk],
                          preferred_element_type=F32)
            for g in range(chunk // LANES):
                a = acc[:, g * LANES:(g + 1) * LANES]
                r = a * c_mul + pltpu.roll(a, LANES // 2, 1) * s_mul
                lo = c * chunk + g * LANES
                z_ref[:, lo:lo + LANES] = r.astype(z_ref.dtype)

    @pl.when(j == 0)
    def _():
        xf = x_ref[...]
        ms = jnp.mean(xf * xf, axis=-1, keepdims=True)
        hb = ((xf * lax.rsqrt(ms + EPS)) * g_ref[...]).astype(BF16)
        h_ref[...] = hb
        fl_ref[...] = jnp.dot(hb, wf_ref[...], preferred_element_type=F32)
        rope_tile(rope_ref[0], rope_ref[1])

    @pl.when(j == 1)
    def _():
        rope_tile(rope_ref[2], rope_ref[3])

    @pl.when(j >= 2)
    def _():
        z_ref[...] = jnp.dot(h_ref[...], w_ref[...],
                             preferred_element_type=F32).astype(z_ref.dtype)


def _inproj(x2, g_norm, w_cat, w_f, rope, *, seq, tm, tn, qk_cols):
    m, d = x2.shape
    n = w_cat.shape[1]
    assert qk_cols == tn
    vmem = (2 * tm * d * 4 + tm * d * 2 + 2 * d * tn * 2 + 2 * tm * tn * 2
            + 2 * 4 * tm * LANES * 4 + 2 * tm * LANES * 4 + 2 * d * LANES * 2
            + tm * tn * 4)
    return pl.pallas_call(
        functools.partial(_inproj_kernel, chunk=512),
        grid=(m // tm, n // tn),
        in_specs=[
            pl.BlockSpec((tm, d), lambda i, j: (i, 0)),
            pl.BlockSpec((1, d), lambda i, j: (0, 0)),
            pl.BlockSpec((d, tn), lambda i, j: (0, j)),
            pl.BlockSpec((d, LANES), lambda i, j: (0, 0)),
            pl.BlockSpec((4, tm, LANES), lambda i, j: (0, i % (seq // tm), 0)),
        ],
        out_specs=[
            pl.BlockSpec((tm, tn), lambda i, j: (i, j)),
            pl.BlockSpec((tm, LANES), lambda i, j: (i, 0)),
        ],
        out_shape=[jax.ShapeDtypeStruct((m, n), BF16),
                   jax.ShapeDtypeStruct((m, LANES), F32)],
        scratch_shapes=[pltpu.VMEM((tm, d), BF16)],
        compiler_params=pltpu.CompilerParams(
            dimension_semantics=("arbitrary", "arbitrary"),
            vmem_limit_bytes=vmem + 6 * MIB),
        name="inproj",
    )(x2, g_norm, w_cat, w_f, rope)


def _forget_cumsum_kernel(fl_ref, b_ref, c_ref, *, heads):
    t = fl_ref[...] + b_ref[...]
    lf = jnp.minimum(t, 0.0) - jnp.log1p(jnp.exp(-jnp.abs(t)))
    seq = lf.shape[0]
    row = lax.broadcasted_iota(jnp.int32, lf.shape, 0)
    d = 1
    while d < seq:
        lf = lf + jnp.where(row >= d, pltpu.roll(lf, d, 0), 0.0)
        d *= 2
    c_ref[0] = lf.T[:heads, :]


def _forget_cumsum(fl, b_pad, *, batch, seq, heads):
    return pl.pallas_call(
        functools.partial(_forget_cumsum_kernel, heads=heads),
        grid=(batch,),
        in_specs=[pl.BlockSpec((seq, LANES), lambda b: (b, 0)),
                  pl.BlockSpec((1, LANES), lambda b: (0, 0))],
        out_specs=pl.BlockSpec((1, heads, seq), lambda b: (b, 0, 0)),
        out_shape=jax.ShapeDtypeStruct((batch, heads, seq), F32),
        compiler_params=pltpu.CompilerParams(dimension_semantics=("arbitrary",)),
        name="forget_cumsum",
    )(fl, b_pad)


def _lane_group_max(s):
    groups = [s[:, c * LANES:(c + 1) * LANES] for c in range(s.shape[1] // LANES)]
    out = groups[0]
    for g in groups[1:]:
        out = jnp.maximum(out, g)
    return out


def _silu(x):
    return x * (1.0 / (1.0 + jnp.exp(-x)))


def _diff_attn_kernel(q_ref, k_ref, v_ref, g_ref, lq1_ref, lk1_ref, lq2_ref, lk2_ref,
                      gsub_ref, o_ref, s_scr, *, tq):
    seq = q_ref.shape[0]
    rows = 2 * tq
    lam = (jnp.exp(jnp.sum(lq1_ref[...] * lk1_ref[...], axis=-1, keepdims=True))
           - jnp.exp(jnp.sum(lq2_ref[...] * lk2_ref[...], axis=-1, keepdims=True))
           + LAMBDA_INIT)
    map0 = _map0_lanes(lax.broadcasted_iota(jnp.int32, (tq, LANES), 1))
    qpos = lax.broadcasted_iota(jnp.int32, (rows, tq), 0) & (tq - 1)
    kpos = lax.broadcasted_iota(jnp.int32, (rows, tq), 1)
    causal = kpos <= qpos
    for i in range(seq // tq):
        q = q_ref[i * tq:(i + 1) * tq, :]
        zero = jnp.zeros_like(q)
        qq = jnp.concatenate([jnp.where(map0, q, zero), jnp.where(map0, zero, q)], axis=0)
        mrun = None
        for j in range(i + 1):
            kj = k_ref[j * tq:(j + 1) * tq, :]
            s = lax.dot_general(qq, kj, (((1,), (1,)), ((), ())), preferred_element_type=F32)
            if j == i:
                s = jnp.where(causal, s, -jnp.inf)
            s_scr[:, j * tq:(j + 1) * tq] = s
            bm = _lane_group_max(s)
            mrun = bm if mrun is None else jnp.maximum(mrun, bm)
        m = jnp.max(mrun, axis=-1, keepdims=True)
        kv = (i + 1) * tq
        p = jnp.exp(s_scr[:, :kv] - m)
        l = jnp.sum(p, axis=-1, keepdims=True)
        acc = jnp.dot(p.astype(BF16), v_ref[:kv, :], preferred_element_type=F32)
        o_all = acc * (1.0 / l)
        o = o_all[:tq] - lam * o_all[tq:]
        ms = jnp.mean(o * o, axis=-1, keepdims=True)
        on = (o * lax.rsqrt(ms + EPS)) * gsub_ref[...] * (1.0 - LAMBDA_INIT)
        gate = g_ref[i * tq:(i + 1) * tq, :].astype(F32)
        o_ref[i * tq:(i + 1) * tq, :] = (on * _silu(gate)).astype(o_ref.dtype)


def _fox_attn_kernel(q_ref, k_ref, v_ref, g_ref, c_ref, o_ref, s_scr, *, tq, scale):
    seq = q_ref.shape[0]
    h = pl.program_id(1)
    crow = c_ref[0, pl.ds(h, 1), :]
    ccol = jnp.broadcast_to(crow, (LANES, seq)).T
    qpos = lax.broadcasted_iota(jnp.int32, (tq, tq), 0)
    kpos = lax.broadcasted_iota(jnp.int32, (tq, tq), 1)
    causal = kpos <= qpos
    for i in range(seq // tq):
        q = q_ref[i * tq:(i + 1) * tq, :]
        cq = ccol[i * tq:(i + 1) * tq, :]
        mrun = None
        for j in range(i + 1):
            kj = k_ref[j * tq:(j + 1) * tq, :]
            s = lax.dot_general(q, kj, (((1,), (1,)), ((), ())), preferred_element_type=F32)
            parts = []
            for c in range(tq // LANES):
                ck = crow[:, j * tq + c * LANES:j * tq + (c + 1) * LANES]
                parts.append(s[:, c * LANES:(c + 1) * LANES] * scale + (cq - ck))
            s = jnp.concatenate(parts, axis=1)
            if j == i:
                s = jnp.where(causal, s, -jnp.inf)
            s_scr[:, j * tq:(j + 1) * tq] = s
            bm = _lane_group_max(s)
            mrun = bm if mrun is None else jnp.maximum(mrun, bm)
        m = jnp.max(mrun, axis=-1, keepdims=True)
        kv = (i + 1) * tq
        p = jnp.exp(s_scr[:, :kv] - m)
        l = jnp.sum(p, axis=-1, keepdims=True)
        acc = jnp.dot(p.astype(BF16), v_ref[:kv, :], preferred_element_type=F32)
        gate = g_ref[i * tq:(i + 1) * tq, :].astype(F32)
        o_ref[i * tq:(i + 1) * tq, :] = (acc * (1.0 / l) * _silu(gate)).astype(o_ref.dtype)


def _head_spec(seq, col0):
    return pl.BlockSpec((seq, LANES), lambda b, h: (b, col0 // LANES + h))


def _diff_attn(z, lq1, lk1, lq2, lk2, g_subln, *, batch, seq, heads, col0, tq):
    width = heads * DIFF_V_DIM
    vec = pl.BlockSpec((1, DIFF_QK_DIM), lambda b, h: (0, 0))
    return pl.pallas_call(
        functools.partial(_diff_attn_kernel, tq=tq),
        grid=(batch, heads),
        in_specs=[_head_spec(seq, col0), _head_spec(seq, col0 + width),
                  _head_spec(seq, col0 + 2 * width), _head_spec(seq, col0 + 3 * width),
                  vec, vec, vec, vec,
                  pl.BlockSpec((1, DIFF_V_DIM), lambda b, h: (0, 0))],
        out_specs=pl.BlockSpec((seq, LANES), lambda b, h: (b, h)),
        out_shape=jax.ShapeDtypeStruct((batch * seq, width), BF16),
        scratch_shapes=[pltpu.VMEM((2 * tq, seq), F32)],
        compiler_params=pltpu.CompilerParams(
            dimension_semantics=("arbitrary", "arbitrary"), vmem_limit_bytes=48 * MIB),
        name="diff_attn",
    )(z, z, z, z, lq1, lk1, lq2, lk2, g_subln)


def _fox_attn(z, c, *, batch, seq, heads, col0, tq):
    width = heads * FOX_HEAD_DIM
    return pl.pallas_call(
        functools.partial(_fox_attn_kernel, tq=tq, scale=FOX_HEAD_DIM ** -0.5),
        grid=(batch, heads),
        in_specs=[_head_spec(seq, col0), _head_spec(seq, col0 + width),
                  _head_spec(seq, col0 + 2 * width), _head_spec(seq, col0 + 3 * width),
                  pl.BlockSpec((1, heads, seq), lambda b, h: (b, 0, 0))],
        out_specs=pl.BlockSpec((seq, LANES), lambda b, h: (b, h)),
        out_shape=jax.ShapeDtypeStruct((batch * seq, width), BF16),
        scratch_shapes=[pltpu.VMEM((tq, seq), F32)],
        compiler_params=pltpu.CompilerParams(
            dimension_semantics=("arbitrary", "arbitrary"), vmem_limit_bytes=48 * MIB),
        name="fox_attn",
    )(z, z, z, z, c)


def _merge_kernel(oa_ref, ob_ref, wa_ref, wb_ref, ma_ref, mb_ref, m_ref):
    ya = jnp.dot(oa_ref[...], wa_ref[...], preferred_element_type=F32)
    yb = jnp.dot(ob_ref[...], wb_ref[...], preferred_element_type=F32)
    ga = jax.nn.sigmoid(ma_ref[...].astype(F32))
    gb = jax.nn.sigmoid(mb_ref[...].astype(F32))
    m_ref[...] = (ga * ya + gb * yb).astype(m_ref.dtype)


def _merge(oa, ob, wa, wb, z, *, ma_col0, mb_col0, tm, tn):
    m, wdt = oa.shape
    d = wa.shape[1]
    return pl.pallas_call(
        _merge_kernel,
        grid=(m // tm, d // tn),
        in_specs=[pl.BlockSpec((tm, wdt), lambda i, j: (i, 0)),
                  pl.BlockSpec((tm, wdt), lambda i, j: (i, 0)),
                  pl.BlockSpec((wdt, tn), lambda i, j: (0, j)),
                  pl.BlockSpec((wdt, tn), lambda i, j: (0, j)),
                  pl.BlockSpec((tm, tn), lambda i, j: (i, ma_col0 // tn + j)),
                  pl.BlockSpec((tm, tn), lambda i, j: (i, mb_col0 // tn + j))],
        out_specs=pl.BlockSpec((tm, tn), lambda i, j: (i, j)),
        out_shape=jax.ShapeDtypeStruct((m, d), BF16),
        compiler_params=pltpu.CompilerParams(
            dimension_semantics=("arbitrary", "arbitrary"), vmem_limit_bytes=40 * MIB),
        name="merge",
    )(oa, ob, wa, wb, z, z)


def _outproj_kernel(m_ref, w_ref, x_ref, g_ref, o_ref):
    y = x_ref[...] + jnp.dot(m_ref[...], w_ref[...], preferred_element_type=F32)
    ms = jnp.mean(y * y, axis=-1, keepdims=True)
    o_ref[...] = (y * lax.rsqrt(ms + EPS)) * g_ref[...]


def _outproj(mrg, w_out, x2, g_final, *, tm):
    m, d = x2.shape
    return pl.pallas_call(
        _outproj_kernel,
        grid=(m // tm,),
        in_specs=[pl.BlockSpec((tm, d), lambda i: (i, 0)),
                  pl.BlockSpec((d, d), lambda i: (0, 0)),
                  pl.BlockSpec((tm, d), lambda i: (i, 0)),
                  pl.BlockSpec((1, d), lambda i: (0, 0))],
        out_specs=pl.BlockSpec((tm, d), lambda i: (i, 0)),
        out_shape=jax.ShapeDtypeStruct((m, d), F32),
        compiler_params=pltpu.CompilerParams(
            dimension_semantics=("arbitrary",), vmem_limit_bytes=48 * MIB),
        name="outproj",
    )(mrg, w_out, x2, g_final)


def kernel(x, g_norm, w_in, lambda_q1, lambda_k1, lambda_q2, lambda_k2, g_subln, b_forget,
           w_proj_a, w_proj_b, w_out, g_final):
    batch, seq, d = x.shape
    diff_width = w_proj_a.shape[1]
    fox_width = w_proj_b.shape[1]
    diff_heads = diff_width // DIFF_V_DIM
    fox_heads = fox_width // FOX_HEAD_DIM
    main_cols = 4 * diff_width + 4 * fox_width
    assert w_in.shape[0] == 1 and w_in.shape[2] == main_cols + fox_heads + 2 * d

    w_cat, w_f = _pack_w_in(w_in[0], qk_cols=diff_width, main_cols=main_cols, shift=fox_heads,
                            out_cols=main_cols + 2 * d, tn=512)
    wa, wb, wo = _cast3_bf16(w_proj_a[0], w_proj_b[0], w_out[0], steps=4)
    b_pad = jnp.pad(b_forget, ((0, 0), (0, LANES - fox_heads)))

    x2 = x.reshape(batch * seq, d)
    z, fl = _inproj(x2, g_norm, w_cat, w_f, _rope_tables(seq),
                    seq=seq, tm=1024, tn=1024, qk_cols=diff_width)
    c = _forget_cumsum(fl, b_pad, batch=batch, seq=seq, heads=fox_heads)
    oa = _diff_attn(z, lambda_q1, lambda_k1, lambda_q2, lambda_k2, g_subln,
                    batch=batch, seq=seq, heads=diff_heads, col0=0, tq=256)
    ob = _fox_attn(z, c, batch=batch, seq=seq, heads=fox_heads, col0=4 * diff_width, tq=256)
    mrg = _merge(oa, ob, wa, wb, z, ma_col0=main_cols, mb_col0=main_cols + d, tm=512, tn=512)
    out = _outproj(mrg, wo, x2, g_final.reshape(1, d), tm=256)
    return out.reshape(batch, seq, d)
```

```python
import functools
import math

import jax
import jax.numpy as jnp
from jax import lax
from jax.experimental import pallas as pl
from jax.experimental.pallas import tpu as pltpu

F32 = jnp.float32
BF16 = jnp.bfloat16

LANES = 128
DIFF_QK_DIM = 64
DIFF_V_DIM = 128
FOX_HEAD_DIM = 128
ROPE_DIM = 16
ROPE_THETA = 500000.0
EPS = 1e-6
LAMBDA_INIT = 0.8 - 0.6 * math.exp(-0.3 * 0)

MIB = 1024 * 1024


def _rope_tables(seq):
    half = ROPE_DIM // 2
    inv = ROPE_THETA ** (-jnp.arange(half, dtype=F32) * 2.0 / ROPE_DIM)
    ang = jnp.arange(seq).astype(F32)[:, None] * inv[None, :]
    cos, sin = jnp.cos(ang), jnp.sin(ang)
    rest = LANES // 2 - ROPE_DIM
    ones = jnp.ones((seq, rest), F32)
    zeros = jnp.zeros((seq, rest), F32)
    c_mul = jnp.concatenate([cos, cos, ones, cos, cos, ones], axis=1)
    s_mul = jnp.concatenate([-sin, -sin, zeros, sin, sin, zeros], axis=1)
    q_scale = DIFF_QK_DIM ** -0.5
    return jnp.stack([c_mul * q_scale, s_mul * q_scale, c_mul, s_mul])


def _map0_lanes(lane):
    half = ROPE_DIM // 2
    return (lane < half) | ((lane >= ROPE_DIM) & (lane < LANES // 2 + half))


def _pack_w_in_kernel(a_ref, b_ref, o_ref, wf_ref, *, perm_tiles, first_shifted, shift):
    j = pl.program_id(0)
    tn = a_ref.shape[0]
    half = ROPE_DIM // 2
    mid = LANES // 2

    @pl.when(j < perm_tiles)
    def _():
        for hd in range(tn // LANES):
            r = hd * LANES
            o = jnp.concatenate([a_ref[r:r + half, :], a_ref[r + mid:r + mid + half, :],
                                 a_ref[r + ROPE_DIM:r + mid, :], a_ref[r + half:r + ROPE_DIM, :],
                                 a_ref[r + mid + half:r + LANES, :]], axis=0)
            o_ref[r:r + LANES, :] = o.astype(o_ref.dtype)

    @pl.when((j >= perm_tiles) & (j < first_shifted))
    def _():
        o_ref[...] = a_ref[...].astype(o_ref.dtype)

    @pl.when(j >= first_shifted)
    def _():
        o = jnp.concatenate([a_ref[shift:, :], b_ref[...]], axis=0)
        o_ref[...] = o.astype(o_ref.dtype)

    @pl.when(j == first_shifted)
    def _():
        wf_ref[...] = a_ref[:LANES, :].astype(wf_ref.dtype)


def _pack_w_in(wt, *, qk_cols, main_cols, shift, out_cols, tn):
    d = wt.shape[1]
    first_shifted = main_cols // tn
    return pl.pallas_call(
        functools.partial(_pack_w_in_kernel, perm_tiles=2 * qk_cols // tn,
                          first_shifted=first_shifted, shift=shift),
        grid=(out_cols // tn,),
        in_specs=[pl.BlockSpec((tn, d), lambda j: (j, 0)),
                  pl.BlockSpec((shift, d),
                               lambda j: ((tn // shift) * (jnp.maximum(j, first_shifted) + 1), 0))],
        out_specs=[pl.BlockSpec((tn, d), lambda j: (j, 0)),
                   pl.BlockSpec((LANES, d), lambda j: (0, 0))],
        out_shape=[jax.ShapeDtypeStruct((out_cols, d), BF16),
                   jax.ShapeDtypeStruct((LANES, d), BF16)],
        compiler_params=pltpu.CompilerParams(dimension_semantics=("arbitrary",),
                                             vmem_limit_bytes=32 * MIB),
        name="pack_w_in",
    )(wt, wt)


def _cast3_kernel(a_ref, b_ref, c_ref, oa_ref, ob_ref, oc_ref):
    oa_ref[...] = a_ref[...].astype(oa_ref.dtype)
    ob_ref[...] = b_ref[...].astype(ob_ref.dtype)
    oc_ref[...] = c_ref[...].astype(oc_ref.dtype)


def _cast3_bf16(a, b, c, *, steps):
    def spec(w):
        return pl.BlockSpec((w.shape[0] // steps, w.shape[1]), lambda i: (i, 0))
    return pl.pallas_call(
        _cast3_kernel,
        grid=(steps,),
        in_specs=[spec(a), spec(b), spec(c)],
        out_specs=[spec(a), spec(b), spec(c)],
        out_shape=[jax.ShapeDtypeStruct(w.shape, BF16) for w in (a, b, c)],
        compiler_params=pltpu.CompilerParams(dimension_semantics=("arbitrary",),
                                             vmem_limit_bytes=32 * MIB),
        name="cast_weights",
    )(a, b, c)


def _dot_nt(a, b):
    return lax.dot_general(a, b, (((1,), (1,)), ((), ())), preferred_element_type=F32)


def _inproj_kernel(x_ref, g_ref, w_ref, wf_ref, rope_ref, z_ref, fl_ref, h_ref, *, chunk):
    j = pl.program_id(1)
    tn = w_ref.shape[0]

    def rope_tile(c_mul, s_mul):
        for c in range(tn // chunk):
            acc = _dot_nt(h_ref[...], w_ref[c * chunk:(c + 1) * chunk, :])
            for g in range(chunk // LANES):
                a = acc[:, g * LANES:(g + 1) * LANES]
                r = a * c_mul + pltpu.roll(a, LANES // 2, 1) * s_mul
                lo = c * chunk + g * LANES
                z_ref[:, lo:lo + LANES] = r.astype(z_ref.dtype)

    @pl.when(j == 0)
    def _():
        xf = x_ref[...]
        ms = jnp.mean(xf * xf, axis=-1, keepdims=True)
        hb = ((xf * lax.rsqrt(ms + EPS)) * g_ref[...]).astype(BF16)
        h_ref[...] = hb
        fl_ref[...] = _dot_nt(hb, wf_ref[...])
        rope_tile(rope_ref[0], rope_ref[1])

    @pl.when(j == 1)
    def _():
        rope_tile(rope_ref[2], rope_ref[3])

    @pl.when(j >= 2)
    def _():
        z_ref[...] = _dot_nt(h_ref[...], w_ref[...]).astype(z_ref.dtype)


def _inproj(x2, g_norm, w_cat, w_f, rope, *, seq, tm, tn, qk_cols):
    m, d = x2.shape
    n = w_cat.shape[0]
    assert qk_cols == tn
    vmem = (2 * tm * d * 4 + tm * d * 2 + 2 * d * tn * 2 + 2 * tm * tn * 2
            + 2 * 4 * tm * LANES * 4 + 2 * tm * LANES * 4 + 2 * d * LANES * 2
            + tm * tn * 4)
    return pl.pallas_call(
        functools.partial(_inproj_kernel, chunk=512),
        grid=(m // tm, n // tn),
        in_specs=[
            pl.BlockSpec((tm, d), lambda i, j: (i, 0)),
            pl.BlockSpec((1, d), lambda i, j: (0, 0)),
            pl.BlockSpec((tn, d), lambda i, j: (j, 0)),
            pl.BlockSpec((LANES, d), lambda i, j: (0, 0)),
            pl.BlockSpec((4, tm, LANES), lambda i, j: (0, i % (seq // tm), 0)),
        ],
        out_specs=[
            pl.BlockSpec((tm, tn), lambda i, j: (i, j)),
            pl.BlockSpec((tm, LANES), lambda i, j: (i, 0)),
        ],
        out_shape=[jax.ShapeDtypeStruct((m, n), BF16),
                   jax.ShapeDtypeStruct((m, LANES), F32)],
        scratch_shapes=[pltpu.VMEM((tm, d), BF16)],
        compiler_params=pltpu.CompilerParams(
            dimension_semantics=("arbitrary", "arbitrary"),
            vmem_limit_bytes=vmem + 6 * MIB),
        name="inproj",
    )(x2, g_norm, w_cat, w_f, rope)


def _forget_cumsum_kernel(fl_ref, b_ref, c_ref, *, heads):
    t = fl_ref[...] + b_ref[...]
    lf = jnp.minimum(t, 0.0) - jnp.log1p(jnp.exp(-jnp.abs(t)))
    seq = lf.shape[0]
    row = lax.broadcasted_iota(jnp.int32, lf.shape, 0)
    d = 1
    while d < seq:
        lf = lf + jnp.where(row >= d, pltpu.roll(lf, d, 0), 0.0)
        d *= 2
    c_ref[0] = lf.T[:heads, :]


def _forget_cumsum(fl, b_pad, *, batch, seq, heads):
    return pl.pallas_call(
        functools.partial(_forget_cumsum_kernel, heads=heads),
        grid=(batch,),
        in_specs=[pl.BlockSpec((seq, LANES), lambda b: (b, 0)),
                  pl.BlockSpec((1, LANES), lambda b: (0, 0))],
        out_specs=pl.BlockSpec((1, heads, seq), lambda b: (b, 0, 0)),
        out_shape=jax.ShapeDtypeStruct((batch, heads, seq), F32),
        compiler_params=pltpu.CompilerParams(dimension_semantics=("arbitrary",)),
        name="forget_cumsum",
    )(fl, b_pad)


def _lane_group_max(s):
    groups = [s[:, c * LANES:(c + 1) * LANES] for c in range(s.shape[1] // LANES)]
    out = groups[0]
    for g in groups[1:]:
        out = jnp.maximum(out, g)
    return out


def _silu(x):
    return x * (1.0 / (1.0 + jnp.exp(-x)))


def _diff_attn_kernel(q_ref, k_ref, v_ref, g_ref, lq1_ref, lk1_ref, lq2_ref, lk2_ref,
                      gsub_ref, o_ref, s_scr, *, tq):
    seq = q_ref.shape[0]
    rows = 2 * tq
    lam = (jnp.exp(jnp.sum(lq1_ref[...] * lk1_ref[...], axis=-1, keepdims=True))
           - jnp.exp(jnp.sum(lq2_ref[...] * lk2_ref[...], axis=-1, keepdims=True))
           + LAMBDA_INIT)
    map0 = _map0_lanes(lax.broadcasted_iota(jnp.int32, (tq, LANES), 1))
    qpos = lax.broadcasted_iota(jnp.int32, (rows, tq), 0) & (tq - 1)
    kpos = lax.broadcasted_iota(jnp.int32, (rows, tq), 1)
    causal = kpos <= qpos
    for i in range(seq // tq):
        q = q_ref[i * tq:(i + 1) * tq, :]
        zero = jnp.zeros_like(q)
        qq = jnp.concatenate([jnp.where(map0, q, zero), jnp.where(map0, zero, q)], axis=0)
        mrun = None
        for j in range(i + 1):
            kj = k_ref[j * tq:(j + 1) * tq, :]
            s = lax.dot_general(qq, kj, (((1,), (1,)), ((), ())), preferred_element_type=F32)
            if j == i:
                s = jnp.where(causal, s, -jnp.inf)
            s_scr[:, j * tq:(j + 1) * tq] = s
            bm = _lane_group_max(s)
            mrun = bm if mrun is None else jnp.maximum(mrun, bm)
        m = jnp.max(mrun, axis=-1, keepdims=True)
        kv = (i + 1) * tq
        p = jnp.exp(s_scr[:, :kv] - m)
        l = jnp.sum(p, axis=-1, keepdims=True)
        acc = jnp.dot(p.astype(BF16), v_ref[:kv, :], preferred_element_type=F32)
        o_all = acc * (1.0 / l)
        o = o_all[:tq] - lam * o_all[tq:]
        ms = jnp.mean(o * o, axis=-1, keepdims=True)
        on = (o * lax.rsqrt(ms + EPS)) * gsub_ref[...] * (1.0 - LAMBDA_INIT)
        gate = g_ref[i * tq:(i + 1) * tq, :].astype(F32)
        o_ref[i * tq:(i + 1) * tq, :] = (on * _silu(gate)).astype(o_ref.dtype)


def _fox_attn_kernel(q_ref, k_ref, v_ref, g_ref, c_ref, o_ref, s_scr, *, tq, scale):
    seq = q_ref.shape[0]
    h = pl.program_id(1)
    crow = c_ref[0, pl.ds(h, 1), :]
    ccol = jnp.broadcast_to(crow, (LANES, seq)).T
    qpos = lax.broadcasted_iota(jnp.int32, (tq, tq), 0)
    kpos = lax.broadcasted_iota(jnp.int32, (tq, tq), 1)
    causal = kpos <= qpos
    for i in range(seq // tq):
        q = q_ref[i * tq:(i + 1) * tq, :]
        cq = ccol[i * tq:(i + 1) * tq, :]
        mrun = None
        for j in range(i + 1):
            kj = k_ref[j * tq:(j + 1) * tq, :]
            s = lax.dot_general(q, kj, (((1,), (1,)), ((), ())), preferred_element_type=F32)
            parts = []
            for c in range(tq // LANES):
                ck = crow[:, j * tq + c * LANES:j * tq + (c + 1) * LANES]
                parts.append(s[:, c * LANES:(c + 1) * LANES] * scale + (cq - ck))
            s = jnp.concatenate(parts, axis=1)
            if j == i:
                s = jnp.where(causal, s, -jnp.inf)
            s_scr[:, j * tq:(j + 1) * tq] = s
            bm = _lane_group_max(s)
            mrun = bm if mrun is None else jnp.maximum(mrun, bm)
        m = jnp.max(mrun, axis=-1, keepdims=True)
        kv = (i + 1) * tq
        p = jnp.exp(s_scr[:, :kv] - m)
        l = jnp.sum(p, axis=-1, keepdims=True)
        acc = jnp.dot(p.astype(BF16), v_ref[:kv, :], preferred_element_type=F32)
        gate = g_ref[i * tq:(i + 1) * tq, :].astype(F32)
        o_ref[i * tq:(i + 1) * tq, :] = (acc * (1.0 / l) * _silu(gate)).astype(o_ref.dtype)


def _head_spec(seq, col0):
    return pl.BlockSpec((seq, LANES), lambda b, h: (b, col0 // LANES + h))


def _diff_attn(z, lq1, lk1, lq2, lk2, g_subln, *, batch, seq, heads, col0, tq):
    width = heads * DIFF_V_DIM
    vec = pl.BlockSpec((1, DIFF_QK_DIM), lambda b, h: (0, 0))
    return pl.pallas_call(
        functools.partial(_diff_attn_kernel, tq=tq),
        grid=(batch, heads),
        in_specs=[_head_spec(seq, col0), _head_spec(seq, col0 + width),
                  _head_spec(seq, col0 + 2 * width), _head_spec(seq, col0 + 3 * width),
                  vec, vec, vec, vec,
                  pl.BlockSpec((1, DIFF_V_DIM), lambda b, h: (0, 0))],
        out_specs=pl.BlockSpec((seq, LANES), lambda b, h: (b, h)),
        out_shape=jax.ShapeDtypeStruct((batch * seq, width), BF16),
        scratch_shapes=[pltpu.VMEM((2 * tq, seq), F32)],
        compiler_params=pltpu.CompilerParams(
            dimension_semantics=("arbitrary", "arbitrary"), vmem_limit_bytes=48 * MIB),
        name="diff_attn",
    )(z, z, z, z, lq1, lk1, lq2, lk2, g_subln)


def _fox_attn(z, c, *, batch, seq, heads, col0, tq):
    width = heads * FOX_HEAD_DIM
    return pl.pallas_call(
        functools.partial(_fox_attn_kernel, tq=tq, scale=FOX_HEAD_DIM ** -0.5),
        grid=(batch, heads),
        in_specs=[_head_spec(seq, col0), _head_spec(seq, col0 + width),
                  _head_spec(seq, col0 + 2 * width), _head_spec(seq, col0 + 3 * width),
                  pl.BlockSpec((1, heads, seq), lambda b, h: (b, 0, 0))],
        out_specs=pl.BlockSpec((seq, LANES), lambda b, h: (b, h)),
        out_shape=jax.ShapeDtypeStruct((batch * seq, width), BF16),
        scratch_shapes=[pltpu.VMEM((tq, seq), F32)],
        compiler_params=pltpu.CompilerParams(
            dimension_semantics=("arbitrary", "arbitrary"), vmem_limit_bytes=48 * MIB),
        name="fox_attn",
    )(z, z, z, z, c)


def _merge_kernel(oa_ref, ob_ref, wa_ref, wb_ref, ma_ref, mb_ref, m_ref):
    ya = jnp.dot(oa_ref[...], wa_ref[...], preferred_element_type=F32)
    yb = jnp.dot(ob_ref[...], wb_ref[...], preferred_element_type=F32)
    ga = jax.nn.sigmoid(ma_ref[...].astype(F32))
    gb = jax.nn.sigmoid(mb_ref[...].astype(F32))
    m_ref[...] = (ga * ya + gb * yb).astype(m_ref.dtype)


def _merge(oa, ob, wa, wb, z, *, ma_col0, mb_col0, tm, tn):
    m, wdt = oa.shape
    d = wa.shape[1]
    return pl.pallas_call(
        _merge_kernel,
        grid=(m // tm, d // tn),
        in_specs=[pl.BlockSpec((tm, wdt), lambda i, j: (i, 0)),
                  pl.BlockSpec((tm, wdt), lambda i, j: (i, 0)),
                  pl.BlockSpec((wdt, tn), lambda i, j: (0, j)),
                  pl.BlockSpec((wdt, tn), lambda i, j: (0, j)),
                  pl.BlockSpec((tm, tn), lambda i, j: (i, ma_col0 // tn + j)),
                  pl.BlockSpec((tm, tn), lambda i, j: (i, mb_col0 // tn + j))],
        out_specs=pl.BlockSpec((tm, tn), lambda i, j: (i, j)),
        out_shape=jax.ShapeDtypeStruct((m, d), BF16),
        compiler_params=pltpu.CompilerParams(
            dimension_semantics=("arbitrary", "arbitrary"), vmem_limit_bytes=40 * MIB),
        name="merge",
    )(oa, ob, wa, wb, z, z)


def _outproj_kernel(m_ref, w_ref, x_ref, g_ref, o_ref):
    y = x_ref[...] + jnp.dot(m_ref[...], w_ref[...], preferred_element_type=F32)
    ms = jnp.mean(y * y, axis=-1, keepdims=True)
    o_ref[...] = (y * lax.rsqrt(ms + EPS)) * g_ref[...]


def _outproj(mrg, w_out, x2, g_final, *, tm):
    m, d = x2.shape
    return pl.pallas_call(
        _outproj_kernel,
        grid=(m // tm,),
        in_specs=[pl.BlockSpec((tm, d), lambda i: (i, 0)),
                  pl.BlockSpec((d, d), lambda i: (0, 0)),
                  pl.BlockSpec((tm, d), lambda i: (i, 0)),
                  pl.BlockSpec((1, d), lambda i: (0, 0))],
        out_specs=pl.BlockSpec((tm, d), lambda i: (i, 0)),
        out_shape=jax.ShapeDtypeStruct((m, d), F32),
        compiler_params=pltpu.CompilerParams(
            dimension_semantics=("arbitrary",), vmem_limit_bytes=48 * MIB),
        name="outproj",
    )(mrg, w_out, x2, g_final)


def kernel(x, g_norm, w_in, lambda_q1, lambda_k1, lambda_q2, lambda_k2, g_subln, b_forget,
           w_proj_a, w_proj_b, w_out, g_final):
    batch, seq, d = x.shape
    diff_width = w_proj_a.shape[1]
    fox_width = w_proj_b.shape[1]
    diff_heads = diff_width // DIFF_V_DIM
    fox_heads = fox_width // FOX_HEAD_DIM
    main_cols = 4 * diff_width + 4 * fox_width
    assert w_in.shape[0] == 1 and w_in.shape[2] == main_cols + fox_heads + 2 * d

    w_cat, w_f = _pack_w_in(jnp.swapaxes(w_in[0], 0, 1), qk_cols=diff_width, main_cols=main_cols,
                            shift=fox_heads, out_cols=main_cols + 2 * d, tn=512)
    wa, wb, wo = _cast3_bf16(w_proj_a[0], w_proj_b[0], w_out[0], steps=4)
    b_pad = jnp.pad(b_forget, ((0, 0), (0, LANES - fox_heads)))

    x2 = x.reshape(batch * seq, d)
    z, fl = _inproj(x2, g_norm, w_cat, w_f, _rope_tables(seq),
                    seq=seq, tm=1024, tn=1024, qk_cols=diff_width)
    c = _forget_cumsum(fl, b_pad, batch=batch, seq=seq, heads=fox_heads)
    oa = _diff_attn(z, lambda_q1, lambda_k1, lambda_q2, lambda_k2, g_subln,
                    batch=batch, seq=seq, heads=diff_heads, col0=0, tq=256)
    ob = _fox_attn(z, c, batch=batch, seq=seq, heads=fox_heads, col0=4 * diff_width, tq=256)
    mrg = _merge(oa, ob, wa, wb, z, ma_col0=main_cols, mb_col0=main_cols + d, tm=512, tn=512)
    out = _outproj(mrg, wo, x2, g_final.reshape(1, d), tm=256)
    return out.reshape(batch, seq, d)
```

```python
import functools
import math

import jax
import jax.numpy as jnp
from jax import lax
from jax.experimental import pallas as pl
from jax.experimental.pallas import tpu as pltpu

F32 = jnp.float32
BF16 = jnp.bfloat16

LANES = 128
DIFF_QK_DIM = 64
DIFF_V_DIM = 128
FOX_HEAD_DIM = 128
ROPE_DIM = 16
ROPE_THETA = 500000.0
EPS = 1e-6
LAMBDA_INIT = 0.8 - 0.6 * math.exp(-0.3 * 0)

MIB = 1024 * 1024


def _rope_tables(seq):
    half = ROPE_DIM // 2
    inv = ROPE_THETA ** (-jnp.arange(half, dtype=F32) * 2.0 / ROPE_DIM)
    ang = jnp.arange(seq).astype(F32)[:, None] * inv[None, :]
    cos, sin = jnp.cos(ang), jnp.sin(ang)
    rest = LANES // 2 - ROPE_DIM
    ones = jnp.ones((seq, rest), F32)
    zeros = jnp.zeros((seq, rest), F32)
    c_mul = jnp.concatenate([cos, cos, ones, cos, cos, ones], axis=1)
    s_mul = jnp.concatenate([-sin, -sin, zeros, sin, sin, zeros], axis=1)
    q_scale = DIFF_QK_DIM ** -0.5
    return jnp.stack([c_mul * q_scale, s_mul * q_scale, c_mul, s_mul])


def _map0_lanes(lane):
    half = ROPE_DIM // 2
    return (lane < half) | ((lane >= ROPE_DIM) & (lane < LANES // 2 + half))


def _pack_w_in_kernel(a_ref, b_ref, o_ref, wf_ref, *, perm_tiles, first_shifted, shift):
    j = pl.program_id(0)
    tn = a_ref.shape[0]
    half = ROPE_DIM // 2
    mid = LANES // 2

    @pl.when(j < perm_tiles)
    def _():
        for hd in range(tn // LANES):
            r = hd * LANES
            o = jnp.concatenate([a_ref[r:r + half, :], a_ref[r + mid:r + mid + half, :],
                                 a_ref[r + ROPE_DIM:r + mid, :], a_ref[r + half:r + ROPE_DIM, :],
                                 a_ref[r + mid + half:r + LANES, :]], axis=0)
            o_ref[r:r + LANES, :] = o.astype(o_ref.dtype)

    @pl.when((j >= perm_tiles) & (j < first_shifted))
    def _():
        o_ref[...] = a_ref[...].astype(o_ref.dtype)

    @pl.when(j >= first_shifted)
    def _():
        o = jnp.concatenate([a_ref[shift:, :], b_ref[...]], axis=0)
        o_ref[...] = o.astype(o_ref.dtype)

    @pl.when(j == first_shifted)
    def _():
        wf_ref[...] = a_ref[:LANES, :].astype(wf_ref.dtype)


def _pack_w_in(wt, *, qk_cols, main_cols, shift, out_cols, tn):
    d = wt.shape[1]
    first_shifted = main_cols // tn
    return pl.pallas_call(
        functools.partial(_pack_w_in_kernel, perm_tiles=2 * qk_cols // tn,
                          first_shifted=first_shifted, shift=shift),
        grid=(out_cols // tn,),
        in_specs=[pl.BlockSpec((tn, d), lambda j: (j, 0)),
                  pl.BlockSpec((shift, d),
                               lambda j: ((tn // shift) * (jnp.maximum(j, first_shifted) + 1), 0))],
        out_specs=[pl.BlockSpec((tn, d), lambda j: (j, 0)),
                   pl.BlockSpec((LANES, d), lambda j: (0, 0))],
        out_shape=[jax.ShapeDtypeStruct((out_cols, d), BF16),
                   jax.ShapeDtypeStruct((LANES, d), BF16)],
        compiler_params=pltpu.CompilerParams(dimension_semantics=("arbitrary",),
                                             vmem_limit_bytes=32 * MIB),
        name="pack_w_in",
    )(wt, wt)


def _cast3_kernel(a_ref, b_ref, c_ref, oa_ref, ob_ref, oc_ref):
    oa_ref[...] = a_ref[...].astype(oa_ref.dtype)
    ob_ref[...] = b_ref[...].astype(ob_ref.dtype)
    oc_ref[...] = c_ref[...].astype(oc_ref.dtype)


def _cast3_bf16(a, b, c, *, steps):
    def spec(w):
        return pl.BlockSpec((w.shape[0] // steps, w.shape[1]), lambda i: (i, 0))
    return pl.pallas_call(
        _cast3_kernel,
        grid=(steps,),
        in_specs=[spec(a), spec(b), spec(c)],
        out_specs=[spec(a), spec(b), spec(c)],
        out_shape=[jax.ShapeDtypeStruct(w.shape, BF16) for w in (a, b, c)],
        compiler_params=pltpu.CompilerParams(dimension_semantics=("arbitrary",),
                                             vmem_limit_bytes=32 * MIB),
        name="cast_weights",
    )(a, b, c)


def _dot_nt(a, b):
    return lax.dot_general(a, b, (((1,), (1,)), ((), ())), preferred_element_type=F32)


def _inproj_kernel(x_ref, g_ref, w_ref, wf_ref, rope_ref, z_ref, fl_ref, h_ref, *, chunk):
    j = pl.program_id(1)
    tn = w_ref.shape[0]

    def rope_tile(c_mul, s_mul):
        for c in range(tn // chunk):
            acc = _dot_nt(h_ref[...], w_ref[c * chunk:(c + 1) * chunk, :])
            for g in range(chunk // LANES):
                a = acc[:, g * LANES:(g + 1) * LANES]
                r = a * c_mul + pltpu.roll(a, LANES // 2, 1) * s_mul
                lo = c * chunk + g * LANES
                z_ref[:, lo:lo + LANES] = r.astype(z_ref.dtype)

    @pl.when(j == 0)
    def _():
        xf = x_ref[...]
        ms = jnp.mean(xf * xf, axis=-1, keepdims=True)
        hb = ((xf * lax.rsqrt(ms + EPS)) * g_ref[...]).astype(BF16)
        h_ref[...] = hb
        fl_ref[...] = _dot_nt(hb, wf_ref[...])
        rope_tile(rope_ref[0], rope_ref[1])

    @pl.when(j == 1)
    def _():
        rope_tile(rope_ref[2], rope_ref[3])

    @pl.when(j >= 2)
    def _():
        z_ref[...] = _dot_nt(h_ref[...], w_ref[...]).astype(z_ref.dtype)


def _inproj(x2, g_norm, w_cat, w_f, rope, *, seq, tm, tn, qk_cols):
    m, d = x2.shape
    n = w_cat.shape[0]
    assert qk_cols == tn
    vmem = (2 * tm * d * 4 + tm * d * 2 + 2 * d * tn * 2 + 2 * tm * tn * 2
            + 2 * 4 * tm * LANES * 4 + 2 * tm * LANES * 4 + 2 * d * LANES * 2
            + tm * tn * 4)
    return pl.pallas_call(
        functools.partial(_inproj_kernel, chunk=512),
        grid=(m // tm, n // tn),
        in_specs=[
            pl.BlockSpec((tm, d), lambda i, j: (i, 0)),
            pl.BlockSpec((1, d), lambda i, j: (0, 0)),
            pl.BlockSpec((tn, d), lambda i, j: (j, 0)),
            pl.BlockSpec((LANES, d), lambda i, j: (0, 0)),
            pl.BlockSpec((4, tm, LANES), lambda i, j: (0, i % (seq // tm), 0)),
        ],
        out_specs=[
            pl.BlockSpec((tm, tn), lambda i, j: (i, j)),
            pl.BlockSpec((tm, LANES), lambda i, j: (i, 0)),
        ],
        out_shape=[jax.ShapeDtypeStruct((m, n), BF16),
                   jax.ShapeDtypeStruct((m, LANES), F32)],
        scratch_shapes=[pltpu.VMEM((tm, d), BF16)],
        compiler_params=pltpu.CompilerParams(
            dimension_semantics=("arbitrary", "arbitrary"),
            vmem_limit_bytes=vmem + 6 * MIB),
        name="inproj",
    )(x2, g_norm, w_cat, w_f, rope)


def _forget_cumsum_kernel(fl_ref, b_ref, c_ref, *, heads):
    t = fl_ref[...] + b_ref[...]
    lf = jnp.minimum(t, 0.0) - jnp.log1p(jnp.exp(-jnp.abs(t)))
    seq = lf.shape[0]
    row = lax.broadcasted_iota(jnp.int32, lf.shape, 0)
    d = 1
    while d < seq:
        lf = lf + jnp.where(row >= d, pltpu.roll(lf, d, 0), 0.0)
        d *= 2
    c_ref[0] = lf.T[:heads, :]


def _forget_cumsum(fl, b_pad, *, batch, seq, heads):
    return pl.pallas_call(
        functools.partial(_forget_cumsum_kernel, heads=heads),
        grid=(batch,),
        in_specs=[pl.BlockSpec((seq, LANES), lambda b: (b, 0)),
                  pl.BlockSpec((1, LANES), lambda b: (0, 0))],
        out_specs=pl.BlockSpec((1, heads, seq), lambda b: (b, 0, 0)),
        out_shape=jax.ShapeDtypeStruct((batch, heads, seq), F32),
        compiler_params=pltpu.CompilerParams(dimension_semantics=("arbitrary",)),
        name="forget_cumsum",
    )(fl, b_pad)


LOG2E = math.log2(math.e)
ONES_ROWS = 16


def _silu(x):
    return x * (1.0 / (1.0 + jnp.exp(-x)))


def _load_vt(v_ref, vt_scr):
    hd = v_ref.shape[1]
    vt_scr[:hd, :] = v_ref[...].astype(F32).T.astype(vt_scr.dtype)
    vt_scr[hd:, :] = jnp.ones((ONES_ROWS, vt_scr.shape[1]), vt_scr.dtype)


def _attend(qq, k_ref, vt_scr, s_scr, *, n_kv, tk, score_fn):
    r = qq.shape[0]
    hd = vt_scr.shape[0] - ONES_ROWS
    mrun = None
    for j in range(n_kv):
        s = score_fn(_dot_nt(k_ref[j * tk:(j + 1) * tk, :], qq), j)
        s_scr[j * tk:(j + 1) * tk, :] = s
        bm = jnp.max(s.reshape(tk // 8, 8, r), axis=0)
        mrun = bm if mrun is None else jnp.maximum(mrun, bm)
    m = jnp.max(mrun, axis=0, keepdims=True)
    kv = n_kv * tk
    p = jnp.exp2(s_scr[:kv, :] - m).astype(BF16)
    o = jnp.dot(vt_scr[:, :kv], p, preferred_element_type=F32)
    return o[:hd] * (1.0 / o[hd:hd + 1])


def _causal_score_fn(delta, *, tq, tk, i, scale_fn):
    def fn(raw, j):
        s = scale_fn(raw, j)
        off = i * tq - j * tk
        if off < tk - 1:
            s = jnp.where(delta <= off, s, -jnp.inf)
        return s
    return fn


def _diff_attn_kernel(q_ref, k_ref, v_ref, g_ref, lq1_ref, lk1_ref, lq2_ref, lk2_ref,
                      gsub_ref, o_ref, s_scr, vt_scr, *, tq, tk):
    seq = q_ref.shape[0]
    rows = 2 * tq
    lam = (jnp.exp(jnp.sum(lq1_ref[...] * lk1_ref[...], axis=-1, keepdims=True))
           - jnp.exp(jnp.sum(lq2_ref[...] * lk2_ref[...], axis=-1, keepdims=True))
           + LAMBDA_INIT)
    _load_vt(v_ref, vt_scr)
    map0 = _map0_lanes(lax.broadcasted_iota(jnp.int32, (tq, LANES), 1))
    delta = (lax.broadcasted_iota(jnp.int32, (tk, rows), 0)
             - (lax.broadcasted_iota(jnp.int32, (tk, rows), 1) & (tq - 1)))
    for i in range(seq // tq):
        q = q_ref[i * tq:(i + 1) * tq, :]
        zero = jnp.zeros_like(q)
        qq = jnp.concatenate([jnp.where(map0, q, zero), jnp.where(map0, zero, q)], axis=0)
        score_fn = _causal_score_fn(delta, tq=tq, tk=tk, i=i,
                                    scale_fn=lambda raw, j: raw * LOG2E)
        ot = _attend(qq, k_ref, vt_scr, s_scr.at[i % 2], n_kv=(i + 1) * tq // tk, tk=tk,
                     score_fn=score_fn)
        o = (ot[:, :tq] - lam * ot[:, tq:]).T
        ms = jnp.mean(o * o, axis=-1, keepdims=True)
        on = (o * lax.rsqrt(ms + EPS)) * gsub_ref[...] * (1.0 - LAMBDA_INIT)
        gate = g_ref[i * tq:(i + 1) * tq, :].astype(F32)
        o_ref[i * tq:(i + 1) * tq, :] = (on * _silu(gate)).astype(o_ref.dtype)


def _fox_attn_kernel(q_ref, k_ref, v_ref, g_ref, c_ref, o_ref, s_scr, vt_scr, *, tq, tk, scale):
    seq = q_ref.shape[0]
    h = pl.program_id(1)
    _load_vt(v_ref, vt_scr)
    crow = c_ref[0, pl.ds(h, 1), :] * LOG2E
    ccol = jnp.broadcast_to(crow, (LANES, seq)).T
    delta = (lax.broadcasted_iota(jnp.int32, (tk, tq), 0)
             - lax.broadcasted_iota(jnp.int32, (tk, tq), 1))
    for i in range(seq // tq):
        q = q_ref[i * tq:(i + 1) * tq, :]
        cq = crow[:, i * tq:(i + 1) * tq]

        def scale_fn(raw, j, cq=cq):
            ck = ccol[j * tk:(j + 1) * tk, :]
            parts = [raw[:, g * LANES:(g + 1) * LANES] * (scale * LOG2E)
                     + (cq[:, g * LANES:(g + 1) * LANES] - ck) for g in range(tq // LANES)]
            return jnp.concatenate(parts, axis=1)

        score_fn = _causal_score_fn(delta, tq=tq, tk=tk, i=i, scale_fn=scale_fn)
        ot = _attend(q, k_ref, vt_scr, s_scr.at[i % 2], n_kv=(i + 1) * tq // tk, tk=tk,
                     score_fn=score_fn)
        gate = g_ref[i * tq:(i + 1) * tq, :].astype(F32)
        o_ref[i * tq:(i + 1) * tq, :] = (ot.T * _silu(gate)).astype(o_ref.dtype)


def _head_spec(seq, col0):
    return pl.BlockSpec((seq, LANES), lambda b, h: (b, col0 // LANES + h))


def _diff_attn(z, lq1, lk1, lq2, lk2, g_subln, *, batch, seq, heads, col0, tq, tk):
    width = heads * DIFF_V_DIM
    vec = pl.BlockSpec((1, DIFF_QK_DIM), lambda b, h: (0, 0))
    return pl.pallas_call(
        functools.partial(_diff_attn_kernel, tq=tq, tk=tk),
        grid=(batch, heads),
        in_specs=[_head_spec(seq, col0), _head_spec(seq, col0 + width),
                  _head_spec(seq, col0 + 2 * width), _head_spec(seq, col0 + 3 * width),
                  vec, vec, vec, vec,
                  pl.BlockSpec((1, DIFF_V_DIM), lambda b, h: (0, 0))],
        out_specs=pl.BlockSpec((seq, LANES), lambda b, h: (b, h)),
        out_shape=jax.ShapeDtypeStruct((batch * seq, width), BF16),
        scratch_shapes=[pltpu.VMEM((2, seq, 2 * tq), F32),
                        pltpu.VMEM((DIFF_V_DIM + ONES_ROWS, seq), BF16)],
        compiler_params=pltpu.CompilerParams(
            dimension_semantics=("arbitrary", "arbitrary"), vmem_limit_bytes=48 * MIB),
        name="diff_attn",
    )(z, z, z, z, lq1, lk1, lq2, lk2, g_subln)


def _fox_attn(z, c, *, batch, seq, heads, col0, tq, tk):
    width = heads * FOX_HEAD_DIM
    return pl.pallas_call(
        functools.partial(_fox_attn_kernel, tq=tq, tk=tk, scale=FOX_HEAD_DIM ** -0.5),
        grid=(batch, heads),
        in_specs=[_head_spec(seq, col0), _head_spec(seq, col0 + width),
                  _head_spec(seq, col0 + 2 * width), _head_spec(seq, col0 + 3 * width),
                  pl.BlockSpec((1, heads, seq), lambda b, h: (b, 0, 0))],
        out_specs=pl.BlockSpec((seq, LANES), lambda b, h: (b, h)),
        out_shape=jax.ShapeDtypeStruct((batch * seq, width), BF16),
        scratch_shapes=[pltpu.VMEM((2, seq, tq), F32),
                        pltpu.VMEM((FOX_HEAD_DIM + ONES_ROWS, seq), BF16)],
        compiler_params=pltpu.CompilerParams(
            dimension_semantics=("arbitrary", "arbitrary"), vmem_limit_bytes=48 * MIB),
        name="fox_attn",
    )(z, z, z, z, c)


def _merge_kernel(oa_ref, ob_ref, wa_ref, wb_ref, ma_ref, mb_ref, m_ref):
    ya = jnp.dot(oa_ref[...], wa_ref[...], preferred_element_type=F32)
    yb = jnp.dot(ob_ref[...], wb_ref[...], preferred_element_type=F32)
    ga = jax.nn.sigmoid(ma_ref[...].astype(F32))
    gb = jax.nn.sigmoid(mb_ref[...].astype(F32))
    m_ref[...] = (ga * ya + gb * yb).astype(m_ref.dtype)


def _merge(oa, ob, wa, wb, z, *, ma_col0, mb_col0, tm, tn):
    m, wdt = oa.shape
    d = wa.shape[1]
    return pl.pallas_call(
        _merge_kernel,
        grid=(m // tm, d // tn),
        in_specs=[pl.BlockSpec((tm, wdt), lambda i, j: (i, 0)),
                  pl.BlockSpec((tm, wdt), lambda i, j: (i, 0)),
                  pl.BlockSpec((wdt, tn), lambda i, j: (0, j)),
                  pl.BlockSpec((wdt, tn), lambda i, j: (0, j)),
                  pl.BlockSpec((tm, tn), lambda i, j: (i, ma_col0 // tn + j)),
                  pl.BlockSpec((tm, tn), lambda i, j: (i, mb_col0 // tn + j))],
        out_specs=pl.BlockSpec((tm, tn), lambda i, j: (i, j)),
        out_shape=jax.ShapeDtypeStruct((m, d), BF16),
        compiler_params=pltpu.CompilerParams(
            dimension_semantics=("arbitrary", "arbitrary"), vmem_limit_bytes=40 * MIB),
        name="merge",
    )(oa, ob, wa, wb, z, z)


def _outproj_kernel(m_ref, w_ref, x_ref, g_ref, o_ref):
    y = x_ref[...] + jnp.dot(m_ref[...], w_ref[...], preferred_element_type=F32)
    ms = jnp.mean(y * y, axis=-1, keepdims=True)
    o_ref[...] = (y * lax.rsqrt(ms + EPS)) * g_ref[...]


def _outproj(mrg, w_out, x2, g_final, *, tm):
    m, d = x2.shape
    return pl.pallas_call(
        _outproj_kernel,
        grid=(m // tm,),
        in_specs=[pl.BlockSpec((tm, d), lambda i: (i, 0)),
                  pl.BlockSpec((d, d), lambda i: (0, 0)),
                  pl.BlockSpec((tm, d), lambda i: (i, 0)),
                  pl.BlockSpec((1, d), lambda i: (0, 0))],
        out_specs=pl.BlockSpec((tm, d), lambda i: (i, 0)),
        out_shape=jax.ShapeDtypeStruct((m, d), F32),
        compiler_params=pltpu.CompilerParams(
            dimension_semantics=("arbitrary",), vmem_limit_bytes=48 * MIB),
        name="outproj",
    )(mrg, w_out, x2, g_final)


def kernel(x, g_norm, w_in, lambda_q1, lambda_k1, lambda_q2, lambda_k2, g_subln, b_forget,
           w_proj_a, w_proj_b, w_out, g_final):
    batch, seq, d = x.shape
    diff_width = w_proj_a.shape[1]
    fox_width = w_proj_b.shape[1]
    diff_heads = diff_width // DIFF_V_DIM
    fox_heads = fox_width // FOX_HEAD_DIM
    main_cols = 4 * diff_width + 4 * fox_width
    assert w_in.shape[0] == 1 and w_in.shape[2] == main_cols + fox_heads + 2 * d

    w_cat, w_f = _pack_w_in(jnp.swapaxes(w_in[0], 0, 1), qk_cols=diff_width, main_cols=main_cols,
                            shift=fox_heads, out_cols=main_cols + 2 * d, tn=512)
    wa, wb, wo = _cast3_bf16(w_proj_a[0], w_proj_b[0], w_out[0], steps=4)
    b_pad = jnp.pad(b_forget, ((0, 0), (0, LANES - fox_heads)))

    x2 = x.reshape(batch * seq, d)
    z, fl = _inproj(x2, g_norm, w_cat, w_f, _rope_tables(seq),
                    seq=seq, tm=1024, tn=1024, qk_cols=diff_width)
    c = _forget_cumsum(fl, b_pad, batch=batch, seq=seq, heads=fox_heads)
    oa = _diff_attn(z, lambda_q1, lambda_k1, lambda_q2, lambda_k2, g_subln,
                    batch=batch, seq=seq, heads=diff_heads, col0=0, tq=256, tk=256)
    ob = _fox_attn(z, c, batch=batch, seq=seq, heads=fox_heads, col0=4 * diff_width,
                   tq=512, tk=256)
    mrg = _merge(oa, ob, wa, wb, z, ma_col0=main_cols, mb_col0=main_cols + d, tm=512, tn=512)
    out = _outproj(mrg, wo, x2, g_final.reshape(1, d), tm=256)
    return out.reshape(batch, seq, d)
```

```python
import functools
import math

import jax
import jax.numpy as jnp
import numpy as np
from jax import lax
from jax.experimental import pallas as pl
from jax.experimental.pallas import tpu as pltpu

F32 = jnp.float32
BF16 = jnp.bfloat16

LANES = 128
DIFF_QK_DIM = 64
DIFF_V_DIM = 128
FOX_HEAD_DIM = 128
ROPE_DIM = 16
ROPE_THETA = 500000.0
EPS = 1e-6
LAMBDA_INIT = 0.8 - 0.6 * math.exp(-0.3 * 0)

MIB = 1024 * 1024


def _rope_tables(seq):
    half = ROPE_DIM // 2
    inv = ROPE_THETA ** (-jnp.arange(half, dtype=F32) * 2.0 / ROPE_DIM)
    lane = np.arange(LANES)
    first = lane < ROPE_DIM
    second = (lane >= LANES // 2) & (lane < LANES // 2 + ROPE_DIM)
    sign = jnp.asarray(np.where(first, -1.0, np.where(second, 1.0, 0.0)), F32)
    ang = lax.broadcasted_iota(F32, (seq, LANES), 0) * jnp.tile(inv, LANES // half)[None, :]
    c_mul = jnp.where(jnp.asarray(first | second)[None, :], jnp.cos(ang), 1.0)
    s_mul = jnp.sin(ang) * sign[None, :]
    q_scale = DIFF_QK_DIM ** -0.5
    return jnp.stack([c_mul * q_scale, s_mul * q_scale, c_mul, s_mul])


def _map0_lanes(lane):
    half = ROPE_DIM // 2
    return (lane < half) | ((lane >= ROPE_DIM) & (lane < LANES // 2 + half))


def _pack_w_in_kernel(a_ref, b_ref, o_ref, wf_ref, *, perm_tiles, first_shifted, shift):
    j = pl.program_id(0)
    tn = a_ref.shape[0]
    half = ROPE_DIM // 2
    mid = LANES // 2

    @pl.when(j < perm_tiles)
    def _():
        for hd in range(tn // LANES):
            r = hd * LANES
            o = jnp.concatenate([a_ref[r:r + half, :], a_ref[r + mid:r + mid + half, :],
                                 a_ref[r + ROPE_DIM:r + mid, :], a_ref[r + half:r + ROPE_DIM, :],
                                 a_ref[r + mid + half:r + LANES, :]], axis=0)
            o_ref[r:r + LANES, :] = o.astype(o_ref.dtype)

    @pl.when((j >= perm_tiles) & (j < first_shifted))
    def _():
        o_ref[...] = a_ref[...].astype(o_ref.dtype)

    @pl.when(j >= first_shifted)
    def _():
        o = jnp.concatenate([a_ref[shift:, :], b_ref[...]], axis=0)
        o_ref[...] = o.astype(o_ref.dtype)

    @pl.when(j == first_shifted)
    def _():
        wf_ref[...] = a_ref[:LANES, :].astype(wf_ref.dtype)


def _pack_w_in(wt, *, qk_cols, main_cols, shift, out_cols, tn):
    d = wt.shape[1]
    first_shifted = main_cols // tn
    return pl.pallas_call(
        functools.partial(_pack_w_in_kernel, perm_tiles=2 * qk_cols // tn,
                          first_shifted=first_shifted, shift=shift),
        grid=(out_cols // tn,),
        in_specs=[pl.BlockSpec((tn, d), lambda j: (j, 0)),
                  pl.BlockSpec((shift, d),
                               lambda j: ((tn // shift) * (jnp.maximum(j, first_shifted) + 1), 0))],
        out_specs=[pl.BlockSpec((tn, d), lambda j: (j, 0)),
                   pl.BlockSpec((LANES, d), lambda j: (0, 0))],
        out_shape=[jax.ShapeDtypeStruct((out_cols, d), BF16),
                   jax.ShapeDtypeStruct((LANES, d), BF16)],
        compiler_params=pltpu.CompilerParams(dimension_semantics=("arbitrary",),
                                             vmem_limit_bytes=32 * MIB),
        name="pack_w_in",
    )(wt, wt)


def _cast3_kernel(a_ref, b_ref, c_ref, oa_ref, ob_ref, oc_ref):
    oa_ref[...] = a_ref[...].astype(oa_ref.dtype)
    ob_ref[...] = b_ref[...].astype(ob_ref.dtype)
    oc_ref[...] = c_ref[...].astype(oc_ref.dtype)


def _cast3_bf16(a, b, c, *, steps):
    def spec(w):
        return pl.BlockSpec((w.shape[0] // steps, w.shape[1]), lambda i: (i, 0))
    return pl.pallas_call(
        _cast3_kernel,
        grid=(steps,),
        in_specs=[spec(a), spec(b), spec(c)],
        out_specs=[spec(a), spec(b), spec(c)],
        out_shape=[jax.ShapeDtypeStruct(w.shape, BF16) for w in (a, b, c)],
        compiler_params=pltpu.CompilerParams(dimension_semantics=("arbitrary",),
                                             vmem_limit_bytes=32 * MIB),
        name="cast_weights",
    )(a, b, c)


def _dot_nt(a, b):
    return lax.dot_general(a, b, (((1,), (1,)), ((), ())), preferred_element_type=F32)


def _inproj_kernel(x_ref, g_ref, w_ref, wf_ref, rope_ref, z_ref, fl_ref, h_ref, *, chunk):
    j = pl.program_id(1)
    tn = w_ref.shape[0]

    def rope_tile(c_mul, s_mul):
        for c in range(tn // chunk):
            acc = _dot_nt(h_ref[...], w_ref[c * chunk:(c + 1) * chunk, :])
            for g in range(chunk // LANES):
                a = acc[:, g * LANES:(g + 1) * LANES]
                r = a * c_mul + pltpu.roll(a, LANES // 2, 1) * s_mul
                lo = c * chunk + g * LANES
                z_ref[:, lo:lo + LANES] = r.astype(z_ref.dtype)

    @pl.when(j == 0)
    def _():
        xf = x_ref[...]
        ms = jnp.mean(xf * xf, axis=-1, keepdims=True)
        hb = ((xf * lax.rsqrt(ms + EPS)) * g_ref[...]).astype(BF16)
        h_ref[...] = hb
        fl_ref[...] = _dot_nt(hb, wf_ref[...])
        rope_tile(rope_ref[0], rope_ref[1])

    @pl.when(j == 1)
    def _():
        rope_tile(rope_ref[2], rope_ref[3])

    @pl.when(j >= 2)
    def _():
        z_ref[...] = _dot_nt(h_ref[...], w_ref[...]).astype(z_ref.dtype)


def _inproj(x2, g_norm, w_cat, w_f, rope, *, seq, tm, tn, qk_cols):
    m, d = x2.shape
    n = w_cat.shape[0]
    assert qk_cols == tn
    vmem = (2 * tm * d * 4 + tm * d * 2 + 2 * d * tn * 2 + 2 * tm * tn * 2
            + 2 * 4 * tm * LANES * 4 + 2 * tm * LANES * 4 + 2 * d * LANES * 2
            + tm * tn * 4)
    return pl.pallas_call(
        functools.partial(_inproj_kernel, chunk=512),
        grid=(m // tm, n // tn),
        in_specs=[
            pl.BlockSpec((tm, d), lambda i, j: (i, 0)),
            pl.BlockSpec((1, d), lambda i, j: (0, 0)),
            pl.BlockSpec((tn, d), lambda i, j: (j, 0)),
            pl.BlockSpec((LANES, d), lambda i, j: (0, 0)),
            pl.BlockSpec((4, tm, LANES), lambda i, j: (0, i % (seq // tm), 0)),
        ],
        out_specs=[
            pl.BlockSpec((tm, tn), lambda i, j: (i, j)),
            pl.BlockSpec((tm, LANES), lambda i, j: (i, 0)),
        ],
        out_shape=[jax.ShapeDtypeStruct((m, n), BF16),
                   jax.ShapeDtypeStruct((m, LANES), F32)],
        scratch_shapes=[pltpu.VMEM((tm, d), BF16)],
        compiler_params=pltpu.CompilerParams(
            dimension_semantics=("arbitrary", "arbitrary"),
            vmem_limit_bytes=vmem + 6 * MIB),
        name="inproj",
    )(x2, g_norm, w_cat, w_f, rope)


def _forget_cumsum_kernel(fl_ref, b_ref, c_ref, *, heads):
    t = fl_ref[...] + b_ref[...]
    lf = jnp.minimum(t, 0.0) - jnp.log1p(jnp.exp(-jnp.abs(t)))
    seq = lf.shape[0]
    row = lax.broadcasted_iota(jnp.int32, lf.shape, 0)
    d = 1
    while d < seq:
        lf = lf + jnp.where(row >= d, pltpu.roll(lf, d, 0), 0.0)
        d *= 2
    c_ref[0] = lf.T[:heads, :]


def _forget_cumsum(fl, b_pad, *, batch, seq, heads):
    return pl.pallas_call(
        functools.partial(_forget_cumsum_kernel, heads=heads),
        grid=(batch,),
        in_specs=[pl.BlockSpec((seq, LANES), lambda b: (b, 0)),
                  pl.BlockSpec((1, LANES), lambda b: (0, 0))],
        out_specs=pl.BlockSpec((1, heads, seq), lambda b: (b, 0, 0)),
        out_shape=jax.ShapeDtypeStruct((batch, heads, seq), F32),
        compiler_params=pltpu.CompilerParams(dimension_semantics=("arbitrary",)),
        name="forget_cumsum",
    )(fl, b_pad)


LOG2E = math.log2(math.e)
ONES_ROWS = 16


def _silu(x):
    return x * (1.0 / (1.0 + jnp.exp(-x)))


def _load_vt(v_ref, vt_scr):
    hd = v_ref.shape[1]
    vt_scr[:hd, :] = v_ref[...].astype(F32).T.astype(vt_scr.dtype)
    vt_scr[hd:, :] = jnp.ones((ONES_ROWS, vt_scr.shape[1]), vt_scr.dtype)


def _attend(qq, k_ref, vt_scr, s_scr, *, n_kv, tk, score_fn):
    r = qq.shape[0]
    hd = vt_scr.shape[0] - ONES_ROWS
    mrun = None
    for j in range(n_kv):
        s = score_fn(_dot_nt(k_ref[j * tk:(j + 1) * tk, :], qq), j)
        s_scr[j * tk:(j + 1) * tk, :] = s
        bm = jnp.max(s.reshape(tk // 8, 8, r), axis=0)
        mrun = bm if mrun is None else jnp.maximum(mrun, bm)
    m = jnp.max(mrun, axis=0, keepdims=True)
    kv = n_kv * tk
    p = jnp.exp2(s_scr[:kv, :] - m).astype(BF16)
    o = jnp.dot(vt_scr[:, :kv], p, preferred_element_type=F32)
    return o[:hd] * (1.0 / o[hd:hd + 1])


def _causal_score_fn(delta, *, tq, tk, i, scale_fn):
    def fn(raw, j):
        s = scale_fn(raw, j)
        off = i * tq - j * tk
        if off < tk - 1:
            s = jnp.where(delta <= off, s, -jnp.inf)
        return s
    return fn


def _diff_attn_kernel(q_ref, k_ref, v_ref, g_ref, lq1_ref, lk1_ref, lq2_ref, lk2_ref,
                      gsub_ref, o_ref, s_scr, vt_scr, *, tq, tk):
    seq = q_ref.shape[0]
    rows = 2 * tq
    lam = (jnp.exp(jnp.sum(lq1_ref[...] * lk1_ref[...], axis=-1, keepdims=True))
           - jnp.exp(jnp.sum(lq2_ref[...] * lk2_ref[...], axis=-1, keepdims=True))
           + LAMBDA_INIT)
    _load_vt(v_ref, vt_scr)
    map0 = _map0_lanes(lax.broadcasted_iota(jnp.int32, (tq, LANES), 1))
    delta = (lax.broadcasted_iota(jnp.int32, (tk, rows), 0)
             - (lax.broadcasted_iota(jnp.int32, (tk, rows), 1) & (tq - 1)))
    for i in range(seq // tq):
        q = q_ref[i * tq:(i + 1) * tq, :]
        zero = jnp.zeros_like(q)
        qq = jnp.concatenate([jnp.where(map0, q, zero), jnp.where(map0, zero, q)], axis=0)
        score_fn = _causal_score_fn(delta, tq=tq, tk=tk, i=i,
                                    scale_fn=lambda raw, j: raw * LOG2E)
        ot = _attend(qq, k_ref, vt_scr, s_scr.at[i % 2], n_kv=(i + 1) * tq // tk, tk=tk,
                     score_fn=score_fn)
        o = (ot[:, :tq] - lam * ot[:, tq:]).T
        ms = jnp.mean(o * o, axis=-1, keepdims=True)
        on = (o * lax.rsqrt(ms + EPS)) * gsub_ref[...] * (1.0 - LAMBDA_INIT)
        gate = g_ref[i * tq:(i + 1) * tq, :].astype(F32)
        o_ref[i * tq:(i + 1) * tq, :] = (on * _silu(gate)).astype(o_ref.dtype)


def _fox_attn_kernel(q_ref, k_ref, v_ref, g_ref, c_ref, o_ref, s_scr, vt_scr, *, tq, tk, scale):
    seq = q_ref.shape[0]
    h = pl.program_id(1)
    _load_vt(v_ref, vt_scr)
    crow = c_ref[0, pl.ds(h, 1), :] * LOG2E
    ccol = jnp.broadcast_to(crow, (LANES, seq)).T
    delta = (lax.broadcasted_iota(jnp.int32, (tk, tq), 0)
             - lax.broadcasted_iota(jnp.int32, (tk, tq), 1))
    for i in range(seq // tq):
        q = q_ref[i * tq:(i + 1) * tq, :]
        cq = crow[:, i * tq:(i + 1) * tq]

        def scale_fn(raw, j, cq=cq):
            ck = ccol[j * tk:(j + 1) * tk, :]
            parts = [raw[:, g * LANES:(g + 1) * LANES] * (scale * LOG2E)
                     + (cq[:, g * LANES:(g + 1) * LANES] - ck) for g in range(tq // LANES)]
            return jnp.concatenate(parts, axis=1)

        score_fn = _causal_score_fn(delta, tq=tq, tk=tk, i=i, scale_fn=scale_fn)
        ot = _attend(q, k_ref, vt_scr, s_scr.at[i % 2], n_kv=(i + 1) * tq // tk, tk=tk,
                     score_fn=score_fn)
        gate = g_ref[i * tq:(i + 1) * tq, :].astype(F32)
        o_ref[i * tq:(i + 1) * tq, :] = (ot.T * _silu(gate)).astype(o_ref.dtype)


def _head_spec(seq, col0):
    return pl.BlockSpec((seq, LANES), lambda b, h: (b, col0 // LANES + h))


def _diff_attn(z, lq1, lk1, lq2, lk2, g_subln, *, batch, seq, heads, col0, tq, tk):
    width = heads * DIFF_V_DIM
    vec = pl.BlockSpec((1, DIFF_QK_DIM), lambda b, h: (0, 0))
    return pl.pallas_call(
        functools.partial(_diff_attn_kernel, tq=tq, tk=tk),
        grid=(batch, heads),
        in_specs=[_head_spec(seq, col0), _head_spec(seq, col0 + width),
                  _head_spec(seq, col0 + 2 * width), _head_spec(seq, col0 + 3 * width),
                  vec, vec, vec, vec,
                  pl.BlockSpec((1, DIFF_V_DIM), lambda b, h: (0, 0))],
        out_specs=pl.BlockSpec((seq, LANES), lambda b, h: (b, h)),
        out_shape=jax.ShapeDtypeStruct((batch * seq, width), BF16),
        scratch_shapes=[pltpu.VMEM((2, seq, 2 * tq), F32),
                        pltpu.VMEM((DIFF_V_DIM + ONES_ROWS, seq), BF16)],
        compiler_params=pltpu.CompilerParams(
            dimension_semantics=("arbitrary", "arbitrary"), vmem_limit_bytes=48 * MIB),
        name="diff_attn",
    )(z, z, z, z, lq1, lk1, lq2, lk2, g_subln)


def _fox_attn(z, c, *, batch, seq, heads, col0, tq, tk):
    width = heads * FOX_HEAD_DIM
    return pl.pallas_call(
        functools.partial(_fox_attn_kernel, tq=tq, tk=tk, scale=FOX_HEAD_DIM ** -0.5),
        grid=(batch, heads),
        in_specs=[_head_spec(seq, col0), _head_spec(seq, col0 + width),
                  _head_spec(seq, col0 + 2 * width), _head_spec(seq, col0 + 3 * width),
                  pl.BlockSpec((1, heads, seq), lambda b, h: (b, 0, 0))],
        out_specs=pl.BlockSpec((seq, LANES), lambda b, h: (b, h)),
        out_shape=jax.ShapeDtypeStruct((batch * seq, width), BF16),
        scratch_shapes=[pltpu.VMEM((2, seq, tq), F32),
                        pltpu.VMEM((FOX_HEAD_DIM + ONES_ROWS, seq), BF16)],
        compiler_params=pltpu.CompilerParams(
            dimension_semantics=("arbitrary", "arbitrary"), vmem_limit_bytes=48 * MIB),
        name="fox_attn",
    )(z, z, z, z, c)


def _out_stage_kernel(oa_ref, ob_ref, ma_ref, mb_ref, wa_ref, wb_ref, wo_ref, x_ref, g_ref, o_ref,
                      *, chunk):
    d = wo_ref.shape[0]
    y = x_ref[...]
    for c in range(d // chunk):
        cols = slice(c * chunk, (c + 1) * chunk)
        ya = jnp.dot(oa_ref[...], wa_ref[:, cols], preferred_element_type=F32)
        yb = jnp.dot(ob_ref[...], wb_ref[:, cols], preferred_element_type=F32)
        mrg = (jax.nn.sigmoid(ma_ref[:, cols].astype(F32)) * ya
               + jax.nn.sigmoid(mb_ref[:, cols].astype(F32)) * yb)
        y = y + jnp.dot(mrg.astype(BF16), wo_ref[cols, :], preferred_element_type=F32)
    ms = jnp.mean(y * y, axis=-1, keepdims=True)
    o_ref[...] = (y * lax.rsqrt(ms + EPS)) * g_ref[...]


def _out_stage(oa, ob, z, wa, wb, wo, x2, g_final, *, ma_col0, mb_col0, tm):
    m, d = x2.shape
    wdt = oa.shape[1]

    def resident(shape):
        return pl.BlockSpec(shape, lambda i: (0, 0), pipeline_mode=pl.Buffered(1))

    vmem = (2 * (2 * tm * wdt * 2 + 2 * tm * d * 2 + 2 * tm * d * 4)
            + (2 * wdt * d + d * d) * 2 + 6 * tm * d * 4)
    return pl.pallas_call(
        functools.partial(_out_stage_kernel, chunk=512),
        grid=(m // tm,),
        in_specs=[pl.BlockSpec((tm, wdt), lambda i: (i, 0)),
                  pl.BlockSpec((tm, wdt), lambda i: (i, 0)),
                  pl.BlockSpec((tm, d), lambda i: (i, ma_col0 // d)),
                  pl.BlockSpec((tm, d), lambda i: (i, mb_col0 // d)),
                  resident((wdt, d)), resident((wdt, d)), resident((d, d)),
                  pl.BlockSpec((tm, d), lambda i: (i, 0)),
                  resident((1, d))],
        out_specs=pl.BlockSpec((tm, d), lambda i: (i, 0)),
        out_shape=jax.ShapeDtypeStruct((m, d), F32),
        compiler_params=pltpu.CompilerParams(
            dimension_semantics=("arbitrary",), vmem_limit_bytes=vmem),
        name="out_stage",
    )(oa, ob, z, z, wa, wb, wo, x2, g_final)


def kernel(x, g_norm, w_in, lambda_q1, lambda_k1, lambda_q2, lambda_k2, g_subln, b_forget,
           w_proj_a, w_proj_b, w_out, g_final):
    batch, seq, d = x.shape
    diff_width = w_proj_a.shape[1]
    fox_width = w_proj_b.shape[1]
    diff_heads = diff_width // DIFF_V_DIM
    fox_heads = fox_width // FOX_HEAD_DIM
    main_cols = 4 * diff_width + 4 * fox_width
    assert w_in.shape[0] == 1 and w_in.shape[2] == main_cols + fox_heads + 2 * d

    w_cat, w_f = _pack_w_in(jnp.swapaxes(w_in[0], 0, 1), qk_cols=diff_width, main_cols=main_cols,
                            shift=fox_heads, out_cols=main_cols + 2 * d, tn=512)
    wa, wb, wo = _cast3_bf16(w_proj_a[0], w_proj_b[0], w_out[0], steps=4)
    b_pad = jnp.pad(b_forget, ((0, 0), (0, LANES - fox_heads)))

    x2 = x.reshape(batch * seq, d)
    z, fl = _inproj(x2, g_norm, w_cat, w_f, _rope_tables(seq),
                    seq=seq, tm=1024, tn=1024, qk_cols=diff_width)
    c = _forget_cumsum(fl, b_pad, batch=batch, seq=seq, heads=fox_heads)
    oa = _diff_attn(z, lambda_q1, lambda_k1, lambda_q2, lambda_k2, g_subln,
                    batch=batch, seq=seq, heads=diff_heads, col0=0, tq=256, tk=256)
    ob = _fox_attn(z, c, batch=batch, seq=seq, heads=fox_heads, col0=4 * diff_width,
                   tq=512, tk=256)
    out = _out_stage(oa, ob, z, wa, wb, wo, x2, g_final.reshape(1, d),
                     ma_col0=main_cols, mb_col0=main_cols + d, tm=256)
    return out.reshape(batch, seq, d)
```

```python
import functools
import math

import jax
import jax.numpy as jnp
import numpy as np
from jax import lax
from jax.experimental import pallas as pl
from jax.experimental.pallas import tpu as pltpu

F32 = jnp.float32
BF16 = jnp.bfloat16

LANES = 128
DIFF_QK_DIM = 64
DIFF_V_DIM = 128
FOX_HEAD_DIM = 128
ROPE_DIM = 16
ROPE_THETA = 500000.0
EPS = 1e-6
LAMBDA_INIT = 0.8 - 0.6 * math.exp(-0.3 * 0)
LOG2E = math.log2(math.e)
DIFF_Q_SCALE = DIFF_QK_DIM ** -0.5 * LOG2E
FOX_Q_SCALE = FOX_HEAD_DIM ** -0.5 * LOG2E

MIB = 1024 * 1024


def _rope_tables(seq):
    half = ROPE_DIM // 2
    inv = ROPE_THETA ** (-jnp.arange(half, dtype=F32) * 2.0 / ROPE_DIM)
    lane = np.arange(LANES)
    first = lane < ROPE_DIM
    second = (lane >= LANES // 2) & (lane < LANES // 2 + ROPE_DIM)
    sign = jnp.asarray(np.where(first, -1.0, np.where(second, 1.0, 0.0)), F32)
    ang = lax.broadcasted_iota(F32, (seq, LANES), 0) * jnp.tile(inv, LANES // half)[None, :]
    c_mul = jnp.where(jnp.asarray(first | second)[None, :], jnp.cos(ang), 1.0)
    s_mul = jnp.sin(ang) * sign[None, :]
    return jnp.stack([c_mul * DIFF_Q_SCALE, s_mul * DIFF_Q_SCALE, c_mul, s_mul])


def _map0_lanes(lane):
    half = ROPE_DIM // 2
    return (lane < half) | ((lane >= ROPE_DIM) & (lane < LANES // 2 + half))


def _pack_w_in_kernel(a_ref, b_ref, o_ref, wf_ref, *, perm_tiles, first_shifted, shift):
    j = pl.program_id(0)
    tn = a_ref.shape[0]
    half = ROPE_DIM // 2
    mid = LANES // 2

    @pl.when(j < perm_tiles)
    def _():
        for hd in range(tn // LANES):
            r = hd * LANES
            o = jnp.concatenate([a_ref[r:r + half, :], a_ref[r + mid:r + mid + half, :],
                                 a_ref[r + ROPE_DIM:r + mid, :], a_ref[r + half:r + ROPE_DIM, :],
                                 a_ref[r + mid + half:r + LANES, :]], axis=0)
            o_ref[r:r + LANES, :] = o.astype(o_ref.dtype)

    @pl.when((j >= perm_tiles) & (j < first_shifted))
    def _():
        o_ref[...] = a_ref[...].astype(o_ref.dtype)

    @pl.when(j >= first_shifted)
    def _():
        o = jnp.concatenate([a_ref[shift:, :], b_ref[...]], axis=0)
        o_ref[...] = o.astype(o_ref.dtype)

    @pl.when(j == first_shifted)
    def _():
        wf_ref[...] = a_ref[:LANES, :].astype(wf_ref.dtype)


def _pack_w_in(wt, *, qk_cols, main_cols, shift, out_cols, tn):
    d = wt.shape[1]
    first_shifted = main_cols // tn
    return pl.pallas_call(
        functools.partial(_pack_w_in_kernel, perm_tiles=2 * qk_cols // tn,
                          first_shifted=first_shifted, shift=shift),
        grid=(out_cols // tn,),
        in_specs=[pl.BlockSpec((tn, d), lambda j: (j, 0)),
                  pl.BlockSpec((shift, d),
                               lambda j: ((tn // shift) * (jnp.maximum(j, first_shifted) + 1), 0))],
        out_specs=[pl.BlockSpec((tn, d), lambda j: (j, 0)),
                   pl.BlockSpec((LANES, d), lambda j: (0, 0))],
        out_shape=[jax.ShapeDtypeStruct((out_cols, d), BF16),
                   jax.ShapeDtypeStruct((LANES, d), BF16)],
        compiler_params=pltpu.CompilerParams(dimension_semantics=("arbitrary",),
                                             vmem_limit_bytes=32 * MIB),
        name="pack_w_in",
    )(wt, wt)


def _cast3_kernel(a_ref, b_ref, c_ref, oa_ref, ob_ref, oc_ref):
    oa_ref[...] = a_ref[...].astype(oa_ref.dtype)
    ob_ref[...] = b_ref[...].astype(ob_ref.dtype)
    oc_ref[...] = c_ref[...].astype(oc_ref.dtype)


def _cast3_bf16(a, b, c, *, steps):
    def spec(w):
        return pl.BlockSpec((w.shape[0] // steps, w.shape[1]), lambda i: (i, 0))
    return pl.pallas_call(
        _cast3_kernel,
        grid=(steps,),
        in_specs=[spec(a), spec(b), spec(c)],
        out_specs=[spec(a), spec(b), spec(c)],
        out_shape=[jax.ShapeDtypeStruct(w.shape, BF16) for w in (a, b, c)],
        compiler_params=pltpu.CompilerParams(dimension_semantics=("arbitrary",),
                                             vmem_limit_bytes=32 * MIB),
        name="cast_weights",
    )(a, b, c)


def _dot_nt(a, b):
    return lax.dot_general(a, b, (((1,), (1,)), ((), ())), preferred_element_type=F32)


def _inproj_kernel(x_ref, g_ref, w_ref, wf_ref, rope_ref, z_ref, fl_ref, h_ref, *, chunk,
                   fox_q_tile):
    j = pl.program_id(1)
    tn = w_ref.shape[0]

    def rope_tile(c_mul, s_mul):
        for c in range(tn // chunk):
            acc = _dot_nt(h_ref[...], w_ref[c * chunk:(c + 1) * chunk, :])
            for g in range(chunk // LANES):
                a = acc[:, g * LANES:(g + 1) * LANES]
                r = a * c_mul + pltpu.roll(a, LANES // 2, 1) * s_mul
                lo = c * chunk + g * LANES
                z_ref[:, lo:lo + LANES] = r.astype(z_ref.dtype)

    @pl.when(j == 0)
    def _():
        xf = x_ref[...]
        ms = jnp.mean(xf * xf, axis=-1, keepdims=True)
        hb = ((xf * lax.rsqrt(ms + EPS)) * g_ref[...]).astype(BF16)
        h_ref[...] = hb
        fl_ref[...] = _dot_nt(hb, wf_ref[...])
        rope_tile(rope_ref[0], rope_ref[1])

    @pl.when(j == 1)
    def _():
        rope_tile(rope_ref[2], rope_ref[3])

    @pl.when(j == fox_q_tile)
    def _():
        z_ref[...] = (_dot_nt(h_ref[...], w_ref[...]) * FOX_Q_SCALE).astype(z_ref.dtype)

    @pl.when((j >= 2) & (j != fox_q_tile))
    def _():
        z_ref[...] = _dot_nt(h_ref[...], w_ref[...]).astype(z_ref.dtype)


def _inproj(x2, g_norm, w_cat, w_f, rope, *, seq, tm, tn, qk_cols, fox_q_col0):
    m, d = x2.shape
    n = w_cat.shape[0]
    assert qk_cols == tn
    fox_q_tile = fox_q_col0 // tn
    vmem = (2 * tm * d * 4 + tm * d * 2 + 2 * d * tn * 2 + 2 * tm * tn * 2
            + 2 * 4 * tm * LANES * 4 + 2 * tm * LANES * 4 + 2 * d * LANES * 2
            + tm * tn * 4)
    return pl.pallas_call(
        functools.partial(_inproj_kernel, chunk=512, fox_q_tile=fox_q_tile),
        grid=(m // tm, n // tn),
        in_specs=[
            pl.BlockSpec((tm, d), lambda i, j: (i, 0)),
            pl.BlockSpec((1, d), lambda i, j: (0, 0)),
            pl.BlockSpec((tn, d), lambda i, j: (j, 0)),
            pl.BlockSpec((LANES, d), lambda i, j: (0, 0)),
            pl.BlockSpec((4, tm, LANES), lambda i, j: (0, i % (seq // tm), 0)),
        ],
        out_specs=[
            pl.BlockSpec((tm, tn), lambda i, j: (i, j)),
            pl.BlockSpec((tm, LANES), lambda i, j: (i, 0)),
        ],
        out_shape=[jax.ShapeDtypeStruct((m, n), BF16),
                   jax.ShapeDtypeStruct((m, LANES), F32)],
        scratch_shapes=[pltpu.VMEM((tm, d), BF16)],
        compiler_params=pltpu.CompilerParams(
            dimension_semantics=("arbitrary", "arbitrary"),
            vmem_limit_bytes=vmem + 6 * MIB),
        name="inproj",
    )(x2, g_norm, w_cat, w_f, rope)


def _forget_cumsum_kernel(fl_ref, b_ref, c_ref, *, heads):
    t = fl_ref[...] + b_ref[...]
    lf = jnp.minimum(t, 0.0) - jnp.log1p(jnp.exp(-jnp.abs(t)))
    seq = lf.shape[0]
    row = lax.broadcasted_iota(jnp.int32, lf.shape, 0)
    d = 1
    while d < seq:
        lf = lf + jnp.where(row >= d, pltpu.roll(lf, d, 0), 0.0)
        d *= 2
    c_ref[0] = lf.T[:heads, :]


def _forget_cumsum(fl, b_pad, *, batch, seq, heads):
    return pl.pallas_call(
        functools.partial(_forget_cumsum_kernel, heads=heads),
        grid=(batch,),
        in_specs=[pl.BlockSpec((seq, LANES), lambda b: (b, 0)),
                  pl.BlockSpec((1, LANES), lambda b: (0, 0))],
        out_specs=pl.BlockSpec((1, heads, seq), lambda b: (b, 0, 0)),
        out_shape=jax.ShapeDtypeStruct((batch, heads, seq), F32),
        compiler_params=pltpu.CompilerParams(dimension_semantics=("arbitrary",)),
        name="forget_cumsum",
    )(fl, b_pad)


ONES_ROWS = 16


def _silu(x):
    return x * (1.0 / (1.0 + jnp.exp(-x)))


def _load_vt(v_ref, vt_scr):
    hd = v_ref.shape[1]
    vt_scr[:hd, :] = v_ref[...].astype(F32).T.astype(vt_scr.dtype)
    vt_scr[hd:, :] = jnp.ones((ONES_ROWS, vt_scr.shape[1]), vt_scr.dtype)


def _pipelined_attention(n_blocks, n_kv, score_block, finish, s_bufs, vt_scr, *, tk):
    hd = vt_scr.shape[0] - ONES_ROWS

    def stage_scores(i):
        mrun = None
        for j in range(n_kv(i)):
            s = score_block(i, j)
            s_bufs[i % 2][j * tk:(j + 1) * tk, :] = s
            bm = jnp.max(s.reshape(tk // 8, 8, s.shape[1]), axis=0)
            mrun = bm if mrun is None else jnp.maximum(mrun, bm)
        return jnp.max(mrun, axis=0, keepdims=True)

    m_next = stage_scores(0)
    for i in range(n_blocks):
        m = m_next
        if i + 1 < n_blocks:
            m_next = stage_scores(i + 1)
        kv = n_kv(i) * tk
        p = jnp.exp2(s_bufs[i % 2][:kv, :] - m).astype(BF16)
        acc = jnp.dot(vt_scr[:, :kv], p, preferred_element_type=F32)
        finish(i, acc[:hd] * (1.0 / acc[hd:hd + 1]))


def _causal_mask(s, delta, off, tk):
    if off < tk - 1:
        s = jnp.where(delta <= off, s, -jnp.inf)
    return s


def _diff_attn_kernel(q_ref, k_ref, v_ref, g_ref, lq1_ref, lk1_ref, lq2_ref, lk2_ref,
                      gsub_ref, o_ref, s0_scr, s1_scr, vt_scr, *, tq, tk):
    seq = q_ref.shape[0]
    rows = 2 * tq
    lam = (jnp.exp(jnp.sum(lq1_ref[...] * lk1_ref[...], axis=-1, keepdims=True))
           - jnp.exp(jnp.sum(lq2_ref[...] * lk2_ref[...], axis=-1, keepdims=True))
           + LAMBDA_INIT)
    _load_vt(v_ref, vt_scr)
    map0 = _map0_lanes(lax.broadcasted_iota(jnp.int32, (tq, LANES), 1))
    delta = (lax.broadcasted_iota(jnp.int32, (tk, rows), 0)
             - (lax.broadcasted_iota(jnp.int32, (tk, rows), 1) & (tq - 1)))
    stacked = {}

    def stacked_q(i):
        if i not in stacked:
            q = q_ref[i * tq:(i + 1) * tq, :]
            zero = jnp.zeros_like(q)
            stacked[i] = jnp.concatenate([jnp.where(map0, q, zero), jnp.where(map0, zero, q)],
                                         axis=0)
        return stacked[i]

    def score_block(i, j):
        raw = _dot_nt(k_ref[j * tk:(j + 1) * tk, :], stacked_q(i))
        return _causal_mask(raw, delta, i * tq - j * tk, tk)

    def finish(i, ot):
        o = (ot[:, :tq] - lam * ot[:, tq:]).T
        ms = jnp.mean(o * o, axis=-1, keepdims=True)
        on = (o * lax.rsqrt(ms + EPS)) * gsub_ref[...] * (1.0 - LAMBDA_INIT)
        gate = g_ref[i * tq:(i + 1) * tq, :].astype(F32)
        o_ref[i * tq:(i + 1) * tq, :] = (on * _silu(gate)).astype(o_ref.dtype)

    _pipelined_attention(seq // tq, lambda i: (i + 1) * tq // tk, score_block, finish,
                         (s0_scr, s1_scr), vt_scr, tk=tk)


def _fox_attn_kernel(q_ref, k_ref, v_ref, g_ref, c_ref, o_ref, s0_scr, s1_scr, vt_scr, *, tq, tk):
    seq = q_ref.shape[0]
    h = pl.program_id(1)
    _load_vt(v_ref, vt_scr)
    crow = c_ref[0, pl.ds(h, 1), :] * LOG2E
    ccol = jnp.broadcast_to(crow, (LANES, seq)).T
    delta = (lax.broadcasted_iota(jnp.int32, (tk, tq), 0)
             - lax.broadcasted_iota(jnp.int32, (tk, tq), 1))

    def score_block(i, j):
        raw = _dot_nt(k_ref[j * tk:(j + 1) * tk, :], q_ref[i * tq:(i + 1) * tq, :])
        ck = ccol[j * tk:(j + 1) * tk, :]
        parts = [raw[:, g * LANES:(g + 1) * LANES]
                 + (crow[:, i * tq + g * LANES:i * tq + (g + 1) * LANES] - ck)
                 for g in range(tq // LANES)]
        return _causal_mask(jnp.concatenate(parts, axis=1), delta, i * tq - j * tk, tk)

    def finish(i, ot):
        gate = g_ref[i * tq:(i + 1) * tq, :].astype(F32)
        o_ref[i * tq:(i + 1) * tq, :] = (ot.T * _silu(gate)).astype(o_ref.dtype)

    _pipelined_attention(seq // tq, lambda i: (i + 1) * tq // tk, score_block, finish,
                         (s0_scr, s1_scr), vt_scr, tk=tk)


def _head_spec(seq, col0):
    return pl.BlockSpec((seq, LANES), lambda b, h: (b, col0 // LANES + h))


def _diff_attn(z, lq1, lk1, lq2, lk2, g_subln, *, batch, seq, heads, col0, tq, tk):
    width = heads * DIFF_V_DIM
    vec = pl.BlockSpec((1, DIFF_QK_DIM), lambda b, h: (0, 0))
    return pl.pallas_call(
        functools.partial(_diff_attn_kernel, tq=tq, tk=tk),
        grid=(batch, heads),
        in_specs=[_head_spec(seq, col0), _head_spec(seq, col0 + width),
                  _head_spec(seq, col0 + 2 * width), _head_spec(seq, col0 + 3 * width),
                  vec, vec, vec, vec,
                  pl.BlockSpec((1, DIFF_V_DIM), lambda b, h: (0, 0))],
        out_specs=pl.BlockSpec((seq, LANES), lambda b, h: (b, h)),
        out_shape=jax.ShapeDtypeStruct((batch * seq, width), BF16),
        scratch_shapes=[pltpu.VMEM((seq, 2 * tq), F32), pltpu.VMEM((seq, 2 * tq), F32),
                        pltpu.VMEM((DIFF_V_DIM + ONES_ROWS, seq), BF16)],
        compiler_params=pltpu.CompilerParams(
            dimension_semantics=("arbitrary", "arbitrary"), vmem_limit_bytes=48 * MIB),
        name="diff_attn",
    )(z, z, z, z, lq1, lk1, lq2, lk2, g_subln)


def _fox_attn(z, c, *, batch, seq, heads, col0, tq, tk):
    width = heads * FOX_HEAD_DIM
    return pl.pallas_call(
        functools.partial(_fox_attn_kernel, tq=tq, tk=tk),
        grid=(batch, heads),
        in_specs=[_head_spec(seq, col0), _head_spec(seq, col0 + width),
                  _head_spec(seq, col0 + 2 * width), _head_spec(seq, col0 + 3 * width),
                  pl.BlockSpec((1, heads, seq), lambda b, h: (b, 0, 0))],
        out_specs=pl.BlockSpec((seq, LANES), lambda b, h: (b, h)),
        out_shape=jax.ShapeDtypeStruct((batch * seq, width), BF16),
        scratch_shapes=[pltpu.VMEM((seq, tq), F32), pltpu.VMEM((seq, tq), F32),
                        pltpu.VMEM((FOX_HEAD_DIM + ONES_ROWS, seq), BF16)],
        compiler_params=pltpu.CompilerParams(
            dimension_semantics=("arbitrary", "arbitrary"), vmem_limit_bytes=48 * MIB),
        name="fox_attn",
    )(z, z, z, z, c)


def _out_stage_kernel(oa_ref, ob_ref, ma_ref, mb_ref, wa_ref, wb_ref, wo_ref, x_ref, g_ref, o_ref,
                      *, chunk):
    d = wo_ref.shape[0]
    y = x_ref[...]
    for c in range(d // chunk):
        cols = slice(c * chunk, (c + 1) * chunk)
        ya = jnp.dot(oa_ref[...], wa_ref[:, cols], preferred_element_type=F32)
        yb = jnp.dot(ob_ref[...], wb_ref[:, cols], preferred_element_type=F32)
        mrg = (jax.nn.sigmoid(ma_ref[:, cols].astype(F32)) * ya
               + jax.nn.sigmoid(mb_ref[:, cols].astype(F32)) * yb)
        y = y + jnp.dot(mrg.astype(BF16), wo_ref[cols, :], preferred_element_type=F32)
    ms = jnp.mean(y * y, axis=-1, keepdims=True)
    o_ref[...] = (y * lax.rsqrt(ms + EPS)) * g_ref[...]


def _out_stage(oa, ob, z, wa, wb, wo, x2, g_final, *, ma_col0, mb_col0, tm):
    m, d = x2.shape
    wdt = oa.shape[1]

    def resident(shape):
        return pl.BlockSpec(shape, lambda i: (0, 0), pipeline_mode=pl.Buffered(1))

    vmem = (2 * (2 * tm * wdt * 2 + 2 * tm * d * 2 + 2 * tm * d * 4)
            + (2 * wdt * d + d * d) * 2 + 6 * tm * d * 4)
    return pl.pallas_call(
        functools.partial(_out_stage_kernel, chunk=512),
        grid=(m // tm,),
        in_specs=[pl.BlockSpec((tm, wdt), lambda i: (i, 0)),
                  pl.BlockSpec((tm, wdt), lambda i: (i, 0)),
                  pl.BlockSpec((tm, d), lambda i: (i, ma_col0 // d)),
                  pl.BlockSpec((tm, d), lambda i: (i, mb_col0 // d)),
                  resident((wdt, d)), resident((wdt, d)), resident((d, d)),
                  pl.BlockSpec((tm, d), lambda i: (i, 0)),
                  resident((1, d))],
        out_specs=pl.BlockSpec((tm, d), lambda i: (i, 0)),
        out_shape=jax.ShapeDtypeStruct((m, d), F32),
        compiler_params=pltpu.CompilerParams(
            dimension_semantics=("arbitrary",), vmem_limit_bytes=vmem),
        name="out_stage",
    )(oa, ob, z, z, wa, wb, wo, x2, g_final)


def kernel(x, g_norm, w_in, lambda_q1, lambda_k1, lambda_q2, lambda_k2, g_subln, b_forget,
           w_proj_a, w_proj_b, w_out, g_final):
    batch, seq, d = x.shape
    diff_width = w_proj_a.shape[1]
    fox_width = w_proj_b.shape[1]
    diff_heads = diff_width // DIFF_V_DIM
    fox_heads = fox_width // FOX_HEAD_DIM
    main_cols = 4 * diff_width + 4 * fox_width
    assert w_in.shape[0] == 1 and w_in.shape[2] == main_cols + fox_heads + 2 * d

    w_cat, w_f = _pack_w_in(jnp.swapaxes(w_in[0], 0, 1), qk_cols=diff_width, main_cols=main_cols,
                            shift=fox_heads, out_cols=main_cols + 2 * d, tn=512)
    wa, wb, wo = _cast3_bf16(w_proj_a[0], w_proj_b[0], w_out[0], steps=4)
    b_pad = jnp.pad(b_forget, ((0, 0), (0, LANES - fox_heads)))

    x2 = x.reshape(batch * seq, d)
    z, fl = _inproj(x2, g_norm, w_cat, w_f, _rope_tables(seq),
                    seq=seq, tm=1024, tn=1024, qk_cols=diff_width, fox_q_col0=4 * diff_width)
    c = _forget_cumsum(fl, b_pad, batch=batch, seq=seq, heads=fox_heads)
    oa = _diff_attn(z, lambda_q1, lambda_k1, lambda_q2, lambda_k2, g_subln,
                    batch=batch, seq=seq, heads=diff_heads, col0=0, tq=256, tk=256)
    ob = _fox_attn(z, c, batch=batch, seq=seq, heads=fox_heads, col0=4 * diff_width,
                   tq=512, tk=256)
    out = _out_stage(oa, ob, z, wa, wb, wo, x2, g_final.reshape(1, d),
                     ma_col0=main_cols, mb_col0=main_cols + d, tm=256)
    return out.reshape(batch, seq, d)
```

```python
import functools
import math

import jax
import jax.numpy as jnp
import numpy as np
from jax import lax
from jax.experimental import pallas as pl
from jax.experimental.pallas import tpu as pltpu

F32 = jnp.float32
BF16 = jnp.bfloat16

LANES = 128
DIFF_QK_DIM = 64
DIFF_V_DIM = 128
FOX_HEAD_DIM = 128
ROPE_DIM = 16
ROPE_THETA = 500000.0
EPS = 1e-6
LAMBDA_INIT = 0.8 - 0.6 * math.exp(-0.3 * 0)
LOG2E = math.log2(math.e)
DIFF_Q_SCALE = DIFF_QK_DIM ** -0.5 * LOG2E
FOX_Q_SCALE = FOX_HEAD_DIM ** -0.5 * LOG2E

MIB = 1024 * 1024


def _rope_tables(seq):
    half = ROPE_DIM // 2
    inv = ROPE_THETA ** (-jnp.arange(half, dtype=F32) * 2.0 / ROPE_DIM)
    lane = np.arange(LANES)
    first = lane < ROPE_DIM
    second = (lane >= LANES // 2) & (lane < LANES // 2 + ROPE_DIM)
    sign = jnp.asarray(np.where(first, -1.0, np.where(second, 1.0, 0.0)), F32)
    ang = lax.broadcasted_iota(F32, (seq, LANES), 0) * jnp.tile(inv, LANES // half)[None, :]
    c_mul = jnp.where(jnp.asarray(first | second)[None, :], jnp.cos(ang), 1.0)
    s_mul = jnp.sin(ang) * sign[None, :]
    return jnp.stack([c_mul * DIFF_Q_SCALE, s_mul * DIFF_Q_SCALE, c_mul, s_mul])


def _map0_lanes(lane):
    half = ROPE_DIM // 2
    return (lane < half) | ((lane >= ROPE_DIM) & (lane < LANES // 2 + half))


def _pack_w_in_kernel(a_ref, b_ref, o_ref, wf_ref, *, perm_tiles, first_shifted, shift):
    j = pl.program_id(0)
    tn = a_ref.shape[0]
    half = ROPE_DIM // 2
    mid = LANES // 2

    @pl.when(j < perm_tiles)
    def _():
        for hd in range(tn // LANES):
            r = hd * LANES
            o = jnp.concatenate([a_ref[r:r + half, :], a_ref[r + mid:r + mid + half, :],
                                 a_ref[r + ROPE_DIM:r + mid, :], a_ref[r + half:r + ROPE_DIM, :],
                                 a_ref[r + mid + half:r + LANES, :]], axis=0)
            o_ref[r:r + LANES, :] = o.astype(o_ref.dtype)

    @pl.when((j >= perm_tiles) & (j < first_shifted))
    def _():
        o_ref[...] = a_ref[...].astype(o_ref.dtype)

    @pl.when(j >= first_shifted)
    def _():
        o = jnp.concatenate([a_ref[shift:, :], b_ref[...]], axis=0)
        o_ref[...] = o.astype(o_ref.dtype)

    @pl.when(j == first_shifted)
    def _():
        wf_ref[...] = a_ref[:LANES, :].astype(wf_ref.dtype)


def _pack_w_in(wt, *, qk_cols, main_cols, shift, out_cols, tn):
    d = wt.shape[1]
    first_shifted = main_cols // tn
    return pl.pallas_call(
        functools.partial(_pack_w_in_kernel, perm_tiles=2 * qk_cols // tn,
                          first_shifted=first_shifted, shift=shift),
        grid=(out_cols // tn,),
        in_specs=[pl.BlockSpec((tn, d), lambda j: (j, 0)),
                  pl.BlockSpec((shift, d),
                               lambda j: ((tn // shift) * (jnp.maximum(j, first_shifted) + 1), 0))],
        out_specs=[pl.BlockSpec((tn, d), lambda j: (j, 0)),
                   pl.BlockSpec((LANES, d), lambda j: (0, 0))],
        out_shape=[jax.ShapeDtypeStruct((out_cols, d), BF16),
                   jax.ShapeDtypeStruct((LANES, d), BF16)],
        compiler_params=pltpu.CompilerParams(dimension_semantics=("arbitrary",),
                                             vmem_limit_bytes=32 * MIB),
        name="pack_w_in",
    )(wt, wt)


def _cast3_kernel(a_ref, b_ref, c_ref, oa_ref, ob_ref, oc_ref):
    oa_ref[...] = a_ref[...].astype(oa_ref.dtype)
    ob_ref[...] = b_ref[...].astype(ob_ref.dtype)
    oc_ref[...] = c_ref[...].astype(oc_ref.dtype)


def _cast3_bf16(a, b, c, *, steps):
    def spec(w):
        return pl.BlockSpec((w.shape[0] // steps, w.shape[1]), lambda i: (i, 0))
    return pl.pallas_call(
        _cast3_kernel,
        grid=(steps,),
        in_specs=[spec(a), spec(b), spec(c)],
        out_specs=[spec(a), spec(b), spec(c)],
        out_shape=[jax.ShapeDtypeStruct(w.shape, BF16) for w in (a, b, c)],
        compiler_params=pltpu.CompilerParams(dimension_semantics=("arbitrary",),
                                             vmem_limit_bytes=32 * MIB),
        name="cast_weights",
    )(a, b, c)


def _dot_nt(a, b):
    return lax.dot_general(a, b, (((1,), (1,)), ((), ())), preferred_element_type=F32)


def _inproj_kernel(x_ref, g_ref, w_ref, wf_ref, rope_ref, z_ref, fl_ref, h_ref, *, chunk,
                   fox_q_tile):
    j = pl.program_id(1)
    tn = w_ref.shape[0]

    def rope_tile(c_mul, s_mul):
        for c in range(tn // chunk):
            acc = _dot_nt(h_ref[...], w_ref[c * chunk:(c + 1) * chunk, :])
            for g in range(chunk // LANES):
                a = acc[:, g * LANES:(g + 1) * LANES]
                r = a * c_mul + pltpu.roll(a, LANES // 2, 1) * s_mul
                lo = c * chunk + g * LANES
                z_ref[:, lo:lo + LANES] = r.astype(z_ref.dtype)

    @pl.when(j == 0)
    def _():
        xf = x_ref[...]
        ms = jnp.mean(xf * xf, axis=-1, keepdims=True)
        hb = ((xf * lax.rsqrt(ms + EPS)) * g_ref[...]).astype(BF16)
        h_ref[...] = hb
        fl_ref[...] = _dot_nt(hb, wf_ref[...])
        rope_tile(rope_ref[0], rope_ref[1])

    @pl.when(j == 1)
    def _():
        rope_tile(rope_ref[2], rope_ref[3])

    @pl.when(j == fox_q_tile)
    def _():
        z_ref[...] = (_dot_nt(h_ref[...], w_ref[...]) * FOX_Q_SCALE).astype(z_ref.dtype)

    @pl.when((j >= 2) & (j != fox_q_tile))
    def _():
        z_ref[...] = _dot_nt(h_ref[...], w_ref[...]).astype(z_ref.dtype)


def _inproj(x2, g_norm, w_cat, w_f, rope, *, seq, tm, tn, qk_cols, fox_q_col0):
    m, d = x2.shape
    n = w_cat.shape[0]
    assert qk_cols == tn
    fox_q_tile = fox_q_col0 // tn
    vmem = (2 * tm * d * 4 + tm * d * 2 + 2 * d * tn * 2 + 2 * tm * tn * 2
            + 2 * 4 * tm * LANES * 4 + 2 * tm * LANES * 4 + 2 * d * LANES * 2
            + tm * tn * 4)
    return pl.pallas_call(
        functools.partial(_inproj_kernel, chunk=512, fox_q_tile=fox_q_tile),
        grid=(m // tm, n // tn),
        in_specs=[
            pl.BlockSpec((tm, d), lambda i, j: (i, 0)),
            pl.BlockSpec((1, d), lambda i, j: (0, 0)),
            pl.BlockSpec((tn, d), lambda i, j: (j, 0)),
            pl.BlockSpec((LANES, d), lambda i, j: (0, 0)),
            pl.BlockSpec((4, tm, LANES), lambda i, j: (0, i % (seq // tm), 0)),
        ],
        out_specs=[
            pl.BlockSpec((tm, tn), lambda i, j: (i, j)),
            pl.BlockSpec((tm, LANES), lambda i, j: (i, 0)),
        ],
        out_shape=[jax.ShapeDtypeStruct((m, n), BF16),
                   jax.ShapeDtypeStruct((m, LANES), F32)],
        scratch_shapes=[pltpu.VMEM((tm, d), BF16)],
        compiler_params=pltpu.CompilerParams(
            dimension_semantics=("arbitrary", "arbitrary"),
            vmem_limit_bytes=vmem + 6 * MIB),
        name="inproj",
    )(x2, g_norm, w_cat, w_f, rope)


def _forget_cumsum_kernel(fl_ref, b_ref, c_ref, *, heads):
    t = fl_ref[...] + b_ref[...]
    lf = jnp.minimum(t, 0.0) - jnp.log1p(jnp.exp(-jnp.abs(t)))
    seq = lf.shape[0]
    row = lax.broadcasted_iota(jnp.int32, lf.shape, 0)
    d = 1
    while d < seq:
        lf = lf + jnp.where(row >= d, pltpu.roll(lf, d, 0), 0.0)
        d *= 2
    c_ref[0] = lf.T[:heads, :]


def _forget_cumsum(fl, b_pad, *, batch, seq, heads):
    return pl.pallas_call(
        functools.partial(_forget_cumsum_kernel, heads=heads),
        grid=(batch,),
        in_specs=[pl.BlockSpec((seq, LANES), lambda b: (b, 0)),
                  pl.BlockSpec((1, LANES), lambda b: (0, 0))],
        out_specs=pl.BlockSpec((1, heads, seq), lambda b: (b, 0, 0)),
        out_shape=jax.ShapeDtypeStruct((batch, heads, seq), F32),
        compiler_params=pltpu.CompilerParams(dimension_semantics=("arbitrary",)),
        name="forget_cumsum",
    )(fl, b_pad)


ONES_ROWS = 16


def _silu(x):
    return x * (1.0 / (1.0 + jnp.exp(-x)))


def _load_vt(v_ref, vt_scr):
    hd = v_ref.shape[1]
    vt_scr[:hd, :] = v_ref[...].astype(F32).T.astype(vt_scr.dtype)
    vt_scr[hd:, :] = jnp.ones((ONES_ROWS, vt_scr.shape[1]), vt_scr.dtype)


def _pipelined_attention(n_blocks, n_kv, score_block, finish, s_bufs, vt_scr, *, tk):
    hd = vt_scr.shape[0] - ONES_ROWS

    def stage_scores(i):
        mrun = None
        for j in range(n_kv(i)):
            s = score_block(i, j)
            s_bufs[i % 2][j * tk:(j + 1) * tk, :] = s
            bm = jnp.max(s.reshape(tk // 8, 8, s.shape[1]), axis=0)
            mrun = bm if mrun is None else jnp.maximum(mrun, bm)
        return jnp.max(mrun, axis=0, keepdims=True)

    def probs(i, m):
        return jnp.exp2(s_bufs[i % 2][:n_kv(i) * tk, :] - m).astype(BF16)

    m_next = stage_scores(0)
    p_next = probs(0, m_next)
    if n_blocks > 1:
        m_next = stage_scores(1)
    for i in range(n_blocks):
        p, m = p_next, m_next
        if i + 2 < n_blocks:
            m_next = stage_scores(i + 2)
        if i + 1 < n_blocks:
            p_next = probs(i + 1, m)
        acc = jnp.dot(vt_scr[:, :n_kv(i) * tk], p, preferred_element_type=F32)
        finish(i, acc[:hd] * (1.0 / acc[hd:hd + 1]))


def _causal_mask(s, delta, off, tk):
    if off < tk - 1:
        s = jnp.where(delta <= off, s, -jnp.inf)
    return s


def _diff_attn_kernel(q_ref, k_ref, v_ref, g_ref, lq1_ref, lk1_ref, lq2_ref, lk2_ref,
                      gsub_ref, o_ref, s0_scr, s1_scr, vt_scr, *, tq, tk):
    seq = q_ref.shape[0]
    rows = 2 * tq
    lam = (jnp.exp(jnp.sum(lq1_ref[...] * lk1_ref[...], axis=-1, keepdims=True))
           - jnp.exp(jnp.sum(lq2_ref[...] * lk2_ref[...], axis=-1, keepdims=True))
           + LAMBDA_INIT)
    _load_vt(v_ref, vt_scr)
    map0 = _map0_lanes(lax.broadcasted_iota(jnp.int32, (tq, LANES), 1))
    delta = (lax.broadcasted_iota(jnp.int32, (tk, rows), 0)
             - (lax.broadcasted_iota(jnp.int32, (tk, rows), 1) & (tq - 1)))
    stacked = {}

    def stacked_q(i):
        if i not in stacked:
            q = q_ref[i * tq:(i + 1) * tq, :]
            zero = jnp.zeros_like(q)
            stacked[i] = jnp.concatenate([jnp.where(map0, q, zero), jnp.where(map0, zero, q)],
                                         axis=0)
        return stacked[i]

    def score_block(i, j):
        raw = _dot_nt(k_ref[j * tk:(j + 1) * tk, :], stacked_q(i))
        return _causal_mask(raw, delta, i * tq - j * tk, tk)

    def finish(i, ot):
        o = (ot[:, :tq] - lam * ot[:, tq:]).T
        ms = jnp.mean(o * o, axis=-1, keepdims=True)
        on = (o * lax.rsqrt(ms + EPS)) * gsub_ref[...] * (1.0 - LAMBDA_INIT)
        gate = g_ref[i * tq:(i + 1) * tq, :].astype(F32)
        o_ref[i * tq:(i + 1) * tq, :] = (on * _silu(gate)).astype(o_ref.dtype)

    _pipelined_attention(seq // tq, lambda i: (i + 1) * tq // tk, score_block, finish,
                         (s0_scr, s1_scr), vt_scr, tk=tk)


def _fox_attn_kernel(q_ref, k_ref, v_ref, g_ref, c_ref, o_ref, s0_scr, s1_scr, vt_scr, *, tq, tk):
    seq = q_ref.shape[0]
    h = pl.program_id(1)
    _load_vt(v_ref, vt_scr)
    crow = c_ref[0, pl.ds(h, 1), :] * LOG2E
    ccol = jnp.broadcast_to(crow, (LANES, seq)).T
    delta = (lax.broadcasted_iota(jnp.int32, (tk, tq), 0)
             - lax.broadcasted_iota(jnp.int32, (tk, tq), 1))

    def score_block(i, j):
        raw = _dot_nt(k_ref[j * tk:(j + 1) * tk, :], q_ref[i * tq:(i + 1) * tq, :])
        ck = ccol[j * tk:(j + 1) * tk, :]
        parts = [raw[:, g * LANES:(g + 1) * LANES]
                 + (crow[:, i * tq + g * LANES:i * tq + (g + 1) * LANES] - ck)
                 for g in range(tq // LANES)]
        return _causal_mask(jnp.concatenate(parts, axis=1), delta, i * tq - j * tk, tk)

    def finish(i, ot):
        gate = g_ref[i * tq:(i + 1) * tq, :].astype(F32)
        o_ref[i * tq:(i + 1) * tq, :] = (ot.T * _silu(gate)).astype(o_ref.dtype)

    _pipelined_attention(seq // tq, lambda i: (i + 1) * tq // tk, score_block, finish,
                         (s0_scr, s1_scr), vt_scr, tk=tk)


def _head_spec(seq, col0):
    return pl.BlockSpec((seq, LANES), lambda b, h: (b, col0 // LANES + h))


def _diff_attn(z, lq1, lk1, lq2, lk2, g_subln, *, batch, seq, heads, col0, tq, tk):
    width = heads * DIFF_V_DIM
    vec = pl.BlockSpec((1, DIFF_QK_DIM), lambda b, h: (0, 0))
    return pl.pallas_call(
        functools.partial(_diff_attn_kernel, tq=tq, tk=tk),
        grid=(batch, heads),
        in_specs=[_head_spec(seq, col0), _head_spec(seq, col0 + width),
                  _head_spec(seq, col0 + 2 * width), _head_spec(seq, col0 + 3 * width),
                  vec, vec, vec, vec,
                  pl.BlockSpec((1, DIFF_V_DIM), lambda b, h: (0, 0))],
        out_specs=pl.BlockSpec((seq, LANES), lambda b, h: (b, h)),
        out_shape=jax.ShapeDtypeStruct((batch * seq, width), BF16),
        scratch_shapes=[pltpu.VMEM((seq, 2 * tq), F32), pltpu.VMEM((seq, 2 * tq), F32),
                        pltpu.VMEM((DIFF_V_DIM + ONES_ROWS, seq), BF16)],
        compiler_params=pltpu.CompilerParams(
            dimension_semantics=("arbitrary", "arbitrary"), vmem_limit_bytes=48 * MIB),
        name="diff_attn",
    )(z, z, z, z, lq1, lk1, lq2, lk2, g_subln)


def _fox_attn(z, c, *, batch, seq, heads, col0, tq, tk):
    width = heads * FOX_HEAD_DIM
    return pl.pallas_call(
        functools.partial(_fox_attn_kernel, tq=tq, tk=tk),
        grid=(batch, heads),
        in_specs=[_head_spec(seq, col0), _head_spec(seq, col0 + width),
                  _head_spec(seq, col0 + 2 * width), _head_spec(seq, col0 + 3 * width),
                  pl.BlockSpec((1, heads, seq), lambda b, h: (b, 0, 0))],
        out_specs=pl.BlockSpec((seq, LANES), lambda b, h: (b, h)),
        out_shape=jax.ShapeDtypeStruct((batch * seq, width), BF16),
        scratch_shapes=[pltpu.VMEM((seq, tq), F32), pltpu.VMEM((seq, tq), F32),
                        pltpu.VMEM((FOX_HEAD_DIM + ONES_ROWS, seq), BF16)],
        compiler_params=pltpu.CompilerParams(
            dimension_semantics=("arbitrary", "arbitrary"), vmem_limit_bytes=48 * MIB),
        name="fox_attn",
    )(z, z, z, z, c)


def _out_stage_kernel(oa_ref, ob_ref, ma_ref, mb_ref, wa_ref, wb_ref, wo_ref, x_ref, g_ref, o_ref,
                      *, chunk):
    d = wo_ref.shape[0]
    y = x_ref[...]
    for c in range(d // chunk):
        cols = slice(c * chunk, (c + 1) * chunk)
        ya = jnp.dot(oa_ref[...], wa_ref[:, cols], preferred_element_type=F32)
        yb = jnp.dot(ob_ref[...], wb_ref[:, cols], preferred_element_type=F32)
        mrg = (jax.nn.sigmoid(ma_ref[:, cols].astype(F32)) * ya
               + jax.nn.sigmoid(mb_ref[:, cols].astype(F32)) * yb)
        y = y + jnp.dot(mrg.astype(BF16), wo_ref[cols, :], preferred_element_type=F32)
    ms = jnp.mean(y * y, axis=-1, keepdims=True)
    o_ref[...] = (y * lax.rsqrt(ms + EPS)) * g_ref[...]


def _out_stage(oa, ob, z, wa, wb, wo, x2, g_final, *, ma_col0, mb_col0, tm):
    m, d = x2.shape
    wdt = oa.shape[1]

    def resident(shape):
        return pl.BlockSpec(shape, lambda i: (0, 0), pipeline_mode=pl.Buffered(1))

    vmem = (2 * (2 * tm * wdt * 2 + 2 * tm * d * 2 + 2 * tm * d * 4)
            + (2 * wdt * d + d * d) * 2 + 2 * tm * d * 4)
    return pl.pallas_call(
        functools.partial(_out_stage_kernel, chunk=512),
        grid=(m // tm,),
        in_specs=[pl.BlockSpec((tm, wdt), lambda i: (i, 0)),
                  pl.BlockSpec((tm, wdt), lambda i: (i, 0)),
                  pl.BlockSpec((tm, d), lambda i: (i, ma_col0 // d)),
                  pl.BlockSpec((tm, d), lambda i: (i, mb_col0 // d)),
                  resident((wdt, d)), resident((wdt, d)), resident((d, d)),
                  pl.BlockSpec((tm, d), lambda i: (i, 0)),
                  resident((1, d))],
        out_specs=pl.BlockSpec((tm, d), lambda i: (i, 0)),
        out_shape=jax.ShapeDtypeStruct((m, d), F32),
        compiler_params=pltpu.CompilerParams(
            dimension_semantics=("arbitrary",), vmem_limit_bytes=vmem),
        name="out_stage",
    )(oa, ob, z, z, wa, wb, wo, x2, g_final)


def kernel(x, g_norm, w_in, lambda_q1, lambda_k1, lambda_q2, lambda_k2, g_subln, b_forget,
           w_proj_a, w_proj_b, w_out, g_final):
    batch, seq, d = x.shape
    diff_width = w_proj_a.shape[1]
    fox_width = w_proj_b.shape[1]
    diff_heads = diff_width // DIFF_V_DIM
    fox_heads = fox_width // FOX_HEAD_DIM
    main_cols = 4 * diff_width + 4 * fox_width
    assert w_in.shape[0] == 1 and w_in.shape[2] == main_cols + fox_heads + 2 * d

    w_cat, w_f = _pack_w_in(jnp.swapaxes(w_in[0], 0, 1), qk_cols=diff_width, main_cols=main_cols,
                            shift=fox_heads, out_cols=main_cols + 2 * d, tn=512)
    wa, wb, wo = _cast3_bf16(w_proj_a[0], w_proj_b[0], w_out[0], steps=4)
    b_pad = jnp.pad(b_forget, ((0, 0), (0, LANES - fox_heads)))

    x2 = x.reshape(batch * seq, d)
    z, fl = _inproj(x2, g_norm, w_cat, w_f, _rope_tables(seq),
                    seq=seq, tm=1024, tn=1024, qk_cols=diff_width, fox_q_col0=4 * diff_width)
    c = _forget_cumsum(fl, b_pad, batch=batch, seq=seq, heads=fox_heads)
    oa = _diff_attn(z, lambda_q1, lambda_k1, lambda_q2, lambda_k2, g_subln,
                    batch=batch, seq=seq, heads=diff_heads, col0=0, tq=256, tk=256)
    ob = _fox_attn(z, c, batch=batch, seq=seq, heads=fox_heads, col0=4 * diff_width,
                   tq=512, tk=256)
    out = _out_stage(oa, ob, z, wa, wb, wo, x2, g_final.reshape(1, d),
                     ma_col0=main_cols, mb_col0=main_cols + d, tm=512)
    return out.reshape(batch, seq, d)
```

```python
import functools
import math

import jax
import jax.numpy as jnp
import numpy as np
from jax import lax
from jax.experimental import pallas as pl
from jax.experimental.pallas import tpu as pltpu

F32 = jnp.float32
BF16 = jnp.bfloat16

LANES = 128
DIFF_QK_DIM = 64
DIFF_V_DIM = 128
FOX_HEAD_DIM = 128
ROPE_DIM = 16
ROPE_THETA = 500000.0
EPS = 1e-6
LAMBDA_INIT = 0.8 - 0.6 * math.exp(-0.3 * 0)
LOG2E = math.log2(math.e)
DIFF_Q_SCALE = DIFF_QK_DIM ** -0.5 * LOG2E
FOX_Q_SCALE = FOX_HEAD_DIM ** -0.5 * LOG2E

MIB = 1024 * 1024


def _rope_tables(seq):
    half = ROPE_DIM // 2
    inv = ROPE_THETA ** (-jnp.arange(half, dtype=F32) * 2.0 / ROPE_DIM)
    lane = np.arange(LANES)
    first = lane < ROPE_DIM
    second = (lane >= LANES // 2) & (lane < LANES // 2 + ROPE_DIM)
    sign = jnp.asarray(np.where(first, -1.0, np.where(second, 1.0, 0.0)), F32)
    ang = lax.broadcasted_iota(F32, (seq, LANES), 0) * jnp.tile(inv, LANES // half)[None, :]
    c_mul = jnp.where(jnp.asarray(first | second)[None, :], jnp.cos(ang), 1.0)
    s_mul = jnp.sin(ang) * sign[None, :]
    return jnp.stack([c_mul * DIFF_Q_SCALE, s_mul * DIFF_Q_SCALE, c_mul, s_mul])


def _map0_lanes(lane):
    half = ROPE_DIM // 2
    return (lane < half) | ((lane >= ROPE_DIM) & (lane < LANES // 2 + half))


def _cast3_kernel(a_ref, b_ref, c_ref, oa_ref, ob_ref, oc_ref):
    oa_ref[...] = a_ref[...].astype(oa_ref.dtype)
    ob_ref[...] = b_ref[...].astype(ob_ref.dtype)
    oc_ref[...] = c_ref[...].astype(oc_ref.dtype)


def _cast3_bf16(a, b, c, *, steps):
    def spec(w):
        return pl.BlockSpec((w.shape[0] // steps, w.shape[1]), lambda i: (i, 0))
    return pl.pallas_call(
        _cast3_kernel,
        grid=(steps,),
        in_specs=[spec(a), spec(b), spec(c)],
        out_specs=[spec(a), spec(b), spec(c)],
        out_shape=[jax.ShapeDtypeStruct(w.shape, BF16) for w in (a, b, c)],
        compiler_params=pltpu.CompilerParams(dimension_semantics=("arbitrary",),
                                             vmem_limit_bytes=32 * MIB),
        name="cast_weights",
    )(a, b, c)


def _dot_nt(a, b):
    return lax.dot_general(a, b, (((1,), (1,)), ((), ())), preferred_element_type=F32)


def _inproj_kernel(x_ref, g_ref, w_ref, wn_ref, wf_ref, rope_ref, z_ref, fl_ref, h_ref, *, chunk,
                   fox_q_tile, first_shifted):
    j = pl.program_id(1)
    tn = w_ref.shape[0]
    half = ROPE_DIM // 2
    mid = LANES // 2

    def packed_head(r):
        return jnp.concatenate([w_ref[r:r + half, :], w_ref[r + mid:r + mid + half, :],
                                w_ref[r + ROPE_DIM:r + mid, :], w_ref[r + half:r + ROPE_DIM, :],
                                w_ref[r + mid + half:r + LANES, :]], axis=0)

    def rope_tile(c_mul, s_mul):
        for c in range(tn // chunk):
            heads = range(c * chunk // LANES, (c + 1) * chunk // LANES)
            w = jnp.concatenate([packed_head(hd * LANES) for hd in heads], axis=0).astype(BF16)
            acc = _dot_nt(h_ref[...], w)
            for g in range(chunk // LANES):
                a = acc[:, g * LANES:(g + 1) * LANES]
                r = a * c_mul + pltpu.roll(a, LANES // 2, 1) * s_mul
                lo = c * chunk + g * LANES
                z_ref[:, lo:lo + LANES] = r.astype(z_ref.dtype)

    @pl.when(j == 0)
    def _():
        xf = x_ref[...]
        ms = jnp.mean(xf * xf, axis=-1, keepdims=True)
        hb = ((xf * lax.rsqrt(ms + EPS)) * g_ref[...]).astype(BF16)
        h_ref[...] = hb
        wf = jnp.concatenate([wf_ref[...], jnp.zeros((LANES - wf_ref.shape[0], wf_ref.shape[1]),
                                                     wf_ref.dtype)], axis=0)
        fl_ref[...] = _dot_nt(hb, wf.astype(BF16))
        rope_tile(rope_ref[0], rope_ref[1])

    @pl.when(j == 1)
    def _():
        rope_tile(rope_ref[2], rope_ref[3])

    @pl.when(j == fox_q_tile)
    def _():
        acc = _dot_nt(h_ref[...], w_ref[...].astype(BF16))
        z_ref[...] = (acc * FOX_Q_SCALE).astype(z_ref.dtype)

    @pl.when((j >= 2) & (j != fox_q_tile) & (j < first_shifted))
    def _():
        z_ref[...] = _dot_nt(h_ref[...], w_ref[...].astype(BF16)).astype(z_ref.dtype)

    @pl.when(j >= first_shifted)
    def _():
        w = jnp.concatenate([w_ref[wn_ref.shape[0]:, :], wn_ref[...]], axis=0).astype(BF16)
        z_ref[...] = _dot_nt(h_ref[...], w).astype(z_ref.dtype)


def _inproj(x2, g_norm, wt, rope, *, seq, tm, tn, qk_cols, fox_q_col0, main_cols, shift, out_cols):
    m, d = x2.shape
    assert qk_cols == tn
    fox_q_tile = fox_q_col0 // tn
    first_shifted = main_cols // tn
    assert main_cols % tn == 0 and tn % shift == 0 and fox_q_tile < first_shifted
    vmem = (2 * tm * d * 4 + tm * d * 2 + 2 * tn * d * 4 + tn * d * 2 + 2 * tm * tn * 2
            + 2 * 4 * tm * LANES * 4 + 2 * tm * LANES * 4 + tm * tn * 4)
    return pl.pallas_call(
        functools.partial(_inproj_kernel, chunk=512, fox_q_tile=fox_q_tile,
                          first_shifted=first_shifted),
        grid=(m // tm, out_cols // tn),
        in_specs=[
            pl.BlockSpec((tm, d), lambda i, j: (i, 0)),
            pl.BlockSpec((1, d), lambda i, j: (0, 0)),
            pl.BlockSpec((tn, d), lambda i, j: (j, 0)),
            pl.BlockSpec((shift, d),
                         lambda i, j: ((tn // shift) * (jnp.maximum(j, first_shifted) + 1), 0)),
            pl.BlockSpec((shift, d), lambda i, j: (main_cols // shift, 0)),
            pl.BlockSpec((4, tm, LANES), lambda i, j: (0, i % (seq // tm), 0)),
        ],
        out_specs=[
            pl.BlockSpec((tm, tn), lambda i, j: (i, j)),
            pl.BlockSpec((tm, LANES), lambda i, j: (i, 0)),
        ],
        out_shape=[jax.ShapeDtypeStruct((m, out_cols), BF16),
                   jax.ShapeDtypeStruct((m, LANES), F32)],
        scratch_shapes=[pltpu.VMEM((tm, d), BF16)],
        compiler_params=pltpu.CompilerParams(
            dimension_semantics=("arbitrary", "arbitrary"),
            vmem_limit_bytes=vmem + 4 * MIB),
        name="inproj",
    )(x2, g_norm, wt, wt, wt, rope)


def _forget_cumsum_kernel(fl_ref, b_ref, c_ref, *, heads):
    t = fl_ref[...] + b_ref[...]
    lf = jnp.minimum(t, 0.0) - jnp.log1p(jnp.exp(-jnp.abs(t)))
    seq = lf.shape[0]
    row = lax.broadcasted_iota(jnp.int32, lf.shape, 0)
    d = 1
    while d < seq:
        lf = lf + jnp.where(row >= d, pltpu.roll(lf, d, 0), 0.0)
        d *= 2
    c_ref[0] = lf.T[:heads, :]


def _forget_cumsum(fl, b_pad, *, batch, seq, heads):
    return pl.pallas_call(
        functools.partial(_forget_cumsum_kernel, heads=heads),
        grid=(batch,),
        in_specs=[pl.BlockSpec((seq, LANES), lambda b: (b, 0)),
                  pl.BlockSpec((1, LANES), lambda b: (0, 0))],
        out_specs=pl.BlockSpec((1, heads, seq), lambda b: (b, 0, 0)),
        out_shape=jax.ShapeDtypeStruct((batch, heads, seq), F32),
        compiler_params=pltpu.CompilerParams(dimension_semantics=("arbitrary",)),
        name="forget_cumsum",
    )(fl, b_pad)


ONES_ROWS = 16


def _silu(x):
    return x * (1.0 / (1.0 + jnp.exp(-x)))


def _load_vt(v_ref, vt_scr):
    hd = v_ref.shape[1]
    vt_scr[:hd, :] = v_ref[...].astype(F32).T.astype(vt_scr.dtype)
    vt_scr[hd:, :] = jnp.ones((ONES_ROWS, vt_scr.shape[1]), vt_scr.dtype)


def _pipelined_attention(n_blocks, n_kv, score_block, finish, s_bufs, vt_scr, *, tk):
    hd = vt_scr.shape[0] - ONES_ROWS

    def stage_scores(i):
        mrun = None
        for j in range(n_kv(i)):
            s = score_block(i, j)
            s_bufs[i % 2][j * tk:(j + 1) * tk, :] = s
            bm = jnp.max(s.reshape(tk // 8, 8, s.shape[1]), axis=0)
            mrun = bm if mrun is None else jnp.maximum(mrun, bm)
        return jnp.max(mrun, axis=0, keepdims=True)

    def probs(i, m):
        return jnp.exp2(s_bufs[i % 2][:n_kv(i) * tk, :] - m).astype(BF16)

    m_next = stage_scores(0)
    p_next = probs(0, m_next)
    if n_blocks > 1:
        m_next = stage_scores(1)
    for i in range(n_blocks):
        p, m = p_next, m_next
        if i + 2 < n_blocks:
            m_next = stage_scores(i + 2)
        if i + 1 < n_blocks:
            p_next = probs(i + 1, m)
        acc = jnp.dot(vt_scr[:, :n_kv(i) * tk], p, preferred_element_type=F32)
        finish(i, acc[:hd] * (1.0 / acc[hd:hd + 1]))


def _causal_mask(s, delta, off, tk):
    if off < tk - 1:
        s = jnp.where(delta <= off, s, -jnp.inf)
    return s


def _diff_attn_kernel(q_ref, k_ref, v_ref, g_ref, lq1_ref, lk1_ref, lq2_ref, lk2_ref,
                      gsub_ref, o_ref, s0_scr, s1_scr, vt_scr, *, tq, tk):
    seq = q_ref.shape[0]
    rows = 2 * tq
    lam = (jnp.exp(jnp.sum(lq1_ref[...] * lk1_ref[...], axis=-1, keepdims=True))
           - jnp.exp(jnp.sum(lq2_ref[...] * lk2_ref[...], axis=-1, keepdims=True))
           + LAMBDA_INIT)
    _load_vt(v_ref, vt_scr)
    map0 = _map0_lanes(lax.broadcasted_iota(jnp.int32, (tq, LANES), 1))
    delta = (lax.broadcasted_iota(jnp.int32, (tk, rows), 0)
             - (lax.broadcasted_iota(jnp.int32, (tk, rows), 1) & (tq - 1)))
    stacked = {}

    def stacked_q(i):
        if i not in stacked:
            q = q_ref[i * tq:(i + 1) * tq, :]
            zero = jnp.zeros_like(q)
            stacked[i] = jnp.concatenate([jnp.where(map0, q, zero), jnp.where(map0, zero, q)],
                                         axis=0)
        return stacked[i]

    def score_block(i, j):
        raw = _dot_nt(k_ref[j * tk:(j + 1) * tk, :], stacked_q(i))
        return _causal_mask(raw, delta, i * tq - j * tk, tk)

    def finish(i, ot):
        o = (ot[:, :tq] - lam * ot[:, tq:]).T
        ms = jnp.mean(o * o, axis=-1, keepdims=True)
        on = (o * lax.rsqrt(ms + EPS)) * gsub_ref[...] * (1.0 - LAMBDA_INIT)
        gate = g_ref[i * tq:(i + 1) * tq, :].astype(F32)
        o_ref[i * tq:(i + 1) * tq, :] = (on * _silu(gate)).astype(o_ref.dtype)

    _pipelined_attention(seq // tq, lambda i: (i + 1) * tq // tk, score_block, finish,
                         (s0_scr, s1_scr), vt_scr, tk=tk)


def _fox_attn_kernel(q_ref, k_ref, v_ref, g_ref, c_ref, o_ref, s0_scr, s1_scr, vt_scr, *, tq, tk):
    seq = q_ref.shape[0]
    h = pl.program_id(1)
    _load_vt(v_ref, vt_scr)
    crow = c_ref[0, pl.ds(h, 1), :] * LOG2E
    ccol = jnp.broadcast_to(crow, (LANES, seq)).T
    delta = (lax.broadcasted_iota(jnp.int32, (tk, tq), 0)
             - lax.broadcasted_iota(jnp.int32, (tk, tq), 1))

    def score_block(i, j):
        raw = _dot_nt(k_ref[j * tk:(j + 1) * tk, :], q_ref[i * tq:(i + 1) * tq, :])
        ck = ccol[j * tk:(j + 1) * tk, :]
        parts = [raw[:, g * LANES:(g + 1) * LANES]
                 + (crow[:, i * tq + g * LANES:i * tq + (g + 1) * LANES] - ck)
                 for g in range(tq // LANES)]
        return _causal_mask(jnp.concatenate(parts, axis=1), delta, i * tq - j * tk, tk)

    def finish(i, ot):
        gate = g_ref[i * tq:(i + 1) * tq, :].astype(F32)
        o_ref[i * tq:(i + 1) * tq, :] = (ot.T * _silu(gate)).astype(o_ref.dtype)

    _pipelined_attention(seq // tq, lambda i: (i + 1) * tq // tk, score_block, finish,
                         (s0_scr, s1_scr), vt_scr, tk=tk)


def _head_spec(seq, col0):
    return pl.BlockSpec((seq, LANES), lambda b, h: (b, col0 // LANES + h))


def _diff_attn(z, lq1, lk1, lq2, lk2, g_subln, *, batch, seq, heads, col0, tq, tk):
    width = heads * DIFF_V_DIM
    vec = pl.BlockSpec((1, DIFF_QK_DIM), lambda b, h: (0, 0))
    return pl.pallas_call(
        functools.partial(_diff_attn_kernel, tq=tq, tk=tk),
        grid=(batch, heads),
        in_specs=[_head_spec(seq, col0), _head_spec(seq, col0 + width),
                  _head_spec(seq, col0 + 2 * width), _head_spec(seq, col0 + 3 * width),
                  vec, vec, vec, vec,
                  pl.BlockSpec((1, DIFF_V_DIM), lambda b, h: (0, 0))],
        out_specs=pl.BlockSpec((seq, LANES), lambda b, h: (b, h)),
        out_shape=jax.ShapeDtypeStruct((batch * seq, width), BF16),
        scratch_shapes=[pltpu.VMEM((seq, 2 * tq), F32), pltpu.VMEM((seq, 2 * tq), F32),
                        pltpu.VMEM((DIFF_V_DIM + ONES_ROWS, seq), BF16)],
        compiler_params=pltpu.CompilerParams(
            dimension_semantics=("arbitrary", "arbitrary"), vmem_limit_bytes=48 * MIB),
        name="diff_attn",
    )(z, z, z, z, lq1, lk1, lq2, lk2, g_subln)


def _fox_attn(z, c, *, batch, seq, heads, col0, tq, tk):
    width = heads * FOX_HEAD_DIM
    return pl.pallas_call(
        functools.partial(_fox_attn_kernel, tq=tq, tk=tk),
        grid=(batch, heads),
        in_specs=[_head_spec(seq, col0), _head_spec(seq, col0 + width),
                  _head_spec(seq, col0 + 2 * width), _head_spec(seq, col0 + 3 * width),
                  pl.BlockSpec((1, heads, seq), lambda b, h: (b, 0, 0))],
        out_specs=pl.BlockSpec((seq, LANES), lambda b, h: (b, h)),
        out_shape=jax.ShapeDtypeStruct((batch * seq, width), BF16),
        scratch_shapes=[pltpu.VMEM((seq, tq), F32), pltpu.VMEM((seq, tq), F32),
                        pltpu.VMEM((FOX_HEAD_DIM + ONES_ROWS, seq), BF16)],
        compiler_params=pltpu.CompilerParams(
            dimension_semantics=("arbitrary", "arbitrary"), vmem_limit_bytes=48 * MIB),
        name="fox_attn",
    )(z, z, z, z, c)


def _out_stage_kernel(oa_ref, ob_ref, ma_ref, mb_ref, wa_ref, wb_ref, wo_ref, x_ref, g_ref, o_ref,
                      *, chunk):
    d = wo_ref.shape[0]
    y = x_ref[...]
    for c in range(d // chunk):
        cols = slice(c * chunk, (c + 1) * chunk)
        ya = jnp.dot(oa_ref[...], wa_ref[:, cols], preferred_element_type=F32)
        yb = jnp.dot(ob_ref[...], wb_ref[:, cols], preferred_element_type=F32)
        mrg = (jax.nn.sigmoid(ma_ref[:, cols].astype(F32)) * ya
               + jax.nn.sigmoid(mb_ref[:, cols].astype(F32)) * yb)
        y = y + jnp.dot(mrg.astype(BF16), wo_ref[cols, :], preferred_element_type=F32)
    ms = jnp.mean(y * y, axis=-1, keepdims=True)
    o_ref[...] = (y * lax.rsqrt(ms + EPS)) * g_ref[...]


def _out_stage(oa, ob, z, wa, wb, wo, x2, g_final, *, ma_col0, mb_col0, tm):
    m, d = x2.shape
    wdt = oa.shape[1]

    def resident(shape):
        return pl.BlockSpec(shape, lambda i: (0, 0), pipeline_mode=pl.Buffered(1))

    vmem = (2 * (2 * tm * wdt * 2 + 2 * tm * d * 2 + 2 * tm * d * 4)
            + (2 * wdt * d + d * d) * 2 + 2 * tm * d * 4)
    return pl.pallas_call(
        functools.partial(_out_stage_kernel, chunk=512),
        grid=(m // tm,),
        in_specs=[pl.BlockSpec((tm, wdt), lambda i: (i, 0)),
                  pl.BlockSpec((tm, wdt), lambda i: (i, 0)),
                  pl.BlockSpec((tm, d), lambda i: (i, ma_col0 // d)),
                  pl.BlockSpec((tm, d), lambda i: (i, mb_col0 // d)),
                  resident((wdt, d)), resident((wdt, d)), resident((d, d)),
                  pl.BlockSpec((tm, d), lambda i: (i, 0)),
                  resident((1, d))],
        out_specs=pl.BlockSpec((tm, d), lambda i: (i, 0)),
        out_shape=jax.ShapeDtypeStruct((m, d), F32),
        compiler_params=pltpu.CompilerParams(
            dimension_semantics=("arbitrary",), vmem_limit_bytes=vmem),
        name="out_stage",
    )(oa, ob, z, z, wa, wb, wo, x2, g_final)


def kernel(x, g_norm, w_in, lambda_q1, lambda_k1, lambda_q2, lambda_k2, g_subln, b_forget,
           w_proj_a, w_proj_b, w_out, g_final):
    batch, seq, d = x.shape
    diff_width = w_proj_a.shape[1]
    fox_width = w_proj_b.shape[1]
    diff_heads = diff_width // DIFF_V_DIM
    fox_heads = fox_width // FOX_HEAD_DIM
    main_cols = 4 * diff_width + 4 * fox_width
    assert w_in.shape[0] == 1 and w_in.shape[2] == main_cols + fox_heads + 2 * d

    wa, wb, wo = _cast3_bf16(w_proj_a[0], w_proj_b[0], w_out[0], steps=4)
    b_pad = jnp.pad(b_forget, ((0, 0), (0, LANES - fox_heads)))

    x2 = x.reshape(batch * seq, d)
    z, fl = _inproj(x2, g_norm, jnp.swapaxes(w_in[0], 0, 1), _rope_tables(seq),
                    seq=seq, tm=1024, tn=1024, qk_cols=diff_width, fox_q_col0=4 * diff_width,
                    main_cols=main_cols, shift=fox_heads, out_cols=main_cols + 2 * d)
    c = _forget_cumsum(fl, b_pad, batch=batch, seq=seq, heads=fox_heads)
    oa = _diff_attn(z, lambda_q1, lambda_k1, lambda_q2, lambda_k2, g_subln,
                    batch=batch, seq=seq, heads=diff_heads, col0=0, tq=256, tk=256)
    ob = _fox_attn(z, c, batch=batch, seq=seq, heads=fox_heads, col0=4 * diff_width,
                   tq=512, tk=256)
    out = _out_stage(oa, ob, z, wa, wb, wo, x2, g_final.reshape(1, d),
                     ma_col0=main_cols, mb_col0=main_cols + d, tm=512)
    return out.reshape(batch, seq, d)
```

```python
import functools
import math

import jax
import jax.numpy as jnp
import numpy as np
from jax import lax
from jax.experimental import pallas as pl
from jax.experimental.pallas import tpu as pltpu

F32 = jnp.float32
BF16 = jnp.bfloat16

LANES = 128
DIFF_QK_DIM = 64
DIFF_V_DIM = 128
FOX_HEAD_DIM = 128
ROPE_DIM = 16
ROPE_THETA = 500000.0
EPS = 1e-6
LAMBDA_INIT = 0.8 - 0.6 * math.exp(-0.3 * 0)
LOG2E = math.log2(math.e)
DIFF_Q_SCALE = DIFF_QK_DIM ** -0.5 * LOG2E
FOX_Q_SCALE = FOX_HEAD_DIM ** -0.5 * LOG2E

MIB = 1024 * 1024


def _rope_tables(seq):
    half = ROPE_DIM // 2
    inv = ROPE_THETA ** (-jnp.arange(half, dtype=F32) * 2.0 / ROPE_DIM)
    lane = np.arange(LANES)
    first = lane < ROPE_DIM
    second = (lane >= LANES // 2) & (lane < LANES // 2 + ROPE_DIM)
    sign = jnp.asarray(np.where(first, -1.0, np.where(second, 1.0, 0.0)), F32)
    ang = lax.broadcasted_iota(F32, (seq, LANES), 0) * jnp.tile(inv, LANES // half)[None, :]
    c_mul = jnp.where(jnp.asarray(first | second)[None, :], jnp.cos(ang), 1.0)
    s_mul = jnp.sin(ang) * sign[None, :]
    return jnp.stack([c_mul * DIFF_Q_SCALE, s_mul * DIFF_Q_SCALE, c_mul, s_mul])


def _map0_lanes(lane):
    half = ROPE_DIM // 2
    return (lane < half) | ((lane >= ROPE_DIM) & (lane < LANES // 2 + half))


def _cast3_kernel(a_ref, b_ref, c_ref, oa_ref, ob_ref, oc_ref):
    oa_ref[...] = a_ref[...].astype(oa_ref.dtype)
    ob_ref[...] = b_ref[...].astype(ob_ref.dtype)
    oc_ref[...] = c_ref[...].astype(oc_ref.dtype)


def _cast3_bf16(a, b, c, *, steps):
    def spec(w):
        return pl.BlockSpec((w.shape[0] // steps, w.shape[1]), lambda i: (i, 0))
    return pl.pallas_call(
        _cast3_kernel,
        grid=(steps,),
        in_specs=[spec(a), spec(b), spec(c)],
        out_specs=[spec(a), spec(b), spec(c)],
        out_shape=[jax.ShapeDtypeStruct(w.shape, BF16) for w in (a, b, c)],
        compiler_params=pltpu.CompilerParams(dimension_semantics=("arbitrary",),
                                             vmem_limit_bytes=32 * MIB),
        name="cast_weights",
    )(a, b, c)


def _dot_nt(a, b):
    return lax.dot_general(a, b, (((1,), (1,)), ((), ())), preferred_element_type=F32)


def _inproj_kernel(x_ref, g_ref, w_ref, wn_ref, wf_ref, rope_ref, z_ref, fl_ref, h_ref, *, chunk,
                   fox_q_tile, first_shifted):
    j = pl.program_id(1)
    tn = w_ref.shape[0]
    half = ROPE_DIM // 2
    mid = LANES // 2

    def packed_head(r):
        return jnp.concatenate([w_ref[r:r + half, :], w_ref[r + mid:r + mid + half, :],
                                w_ref[r + ROPE_DIM:r + mid, :], w_ref[r + half:r + ROPE_DIM, :],
                                w_ref[r + mid + half:r + LANES, :]], axis=0)

    def rope_tile(c_mul, s_mul):
        for c in range(tn // chunk):
            heads = range(c * chunk // LANES, (c + 1) * chunk // LANES)
            w = jnp.concatenate([packed_head(hd * LANES) for hd in heads], axis=0).astype(BF16)
            acc = _dot_nt(h_ref[...], w)
            for g in range(chunk // LANES):
                a = acc[:, g * LANES:(g + 1) * LANES]
                r = a * c_mul + pltpu.roll(a, LANES // 2, 1) * s_mul
                lo = c * chunk + g * LANES
                z_ref[:, lo:lo + LANES] = r.astype(z_ref.dtype)

    @pl.when(j == 0)
    def _():
        xf = x_ref[...]
        ms = jnp.mean(xf * xf, axis=-1, keepdims=True)
        hb = ((xf * lax.rsqrt(ms + EPS)) * g_ref[...]).astype(BF16)
        h_ref[...] = hb
        wf = jnp.concatenate([wf_ref[...], jnp.zeros((LANES - wf_ref.shape[0], wf_ref.shape[1]),
                                                     wf_ref.dtype)], axis=0)
        fl_ref[...] = _dot_nt(hb, wf.astype(BF16))
        rope_tile(rope_ref[0], rope_ref[1])

    @pl.when(j == 1)
    def _():
        rope_tile(rope_ref[2], rope_ref[3])

    @pl.when(j == fox_q_tile)
    def _():
        acc = _dot_nt(h_ref[...], w_ref[...].astype(BF16))
        z_ref[...] = (acc * FOX_Q_SCALE).astype(z_ref.dtype)

    @pl.when((j >= 2) & (j != fox_q_tile) & (j < first_shifted))
    def _():
        z_ref[...] = _dot_nt(h_ref[...], w_ref[...].astype(BF16)).astype(z_ref.dtype)

    @pl.when(j >= first_shifted)
    def _():
        w = jnp.concatenate([w_ref[wn_ref.shape[0]:, :], wn_ref[...]], axis=0).astype(BF16)
        z_ref[...] = _dot_nt(h_ref[...], w).astype(z_ref.dtype)


def _inproj(x2, g_norm, wt, rope, *, seq, tm, tn, qk_cols, fox_q_col0, main_cols, shift, out_cols):
    m, d = x2.shape
    assert qk_cols == tn
    fox_q_tile = fox_q_col0 // tn
    first_shifted = main_cols // tn
    assert main_cols % tn == 0 and tn % shift == 0 and fox_q_tile < first_shifted
    vmem = (2 * tm * d * 4 + tm * d * 2 + 2 * tn * d * 4 + tn * d * 2 + 2 * tm * tn * 2
            + 2 * 4 * tm * LANES * 4 + 2 * tm * LANES * 4 + tm * tn * 4)
    return pl.pallas_call(
        functools.partial(_inproj_kernel, chunk=512, fox_q_tile=fox_q_tile,
                          first_shifted=first_shifted),
        grid=(m // tm, out_cols // tn),
        in_specs=[
            pl.BlockSpec((tm, d), lambda i, j: (i, 0)),
            pl.BlockSpec((1, d), lambda i, j: (0, 0)),
            pl.BlockSpec((tn, d), lambda i, j: (j, 0)),
            pl.BlockSpec((shift, d),
                         lambda i, j: ((tn // shift) * (jnp.maximum(j, first_shifted) + 1), 0)),
            pl.BlockSpec((shift, d), lambda i, j: (main_cols // shift, 0)),
            pl.BlockSpec((4, tm, LANES), lambda i, j: (0, i % (seq // tm), 0)),
        ],
        out_specs=[
            pl.BlockSpec((tm, tn), lambda i, j: (i, j)),
            pl.BlockSpec((tm, LANES), lambda i, j: (i, 0)),
        ],
        out_shape=[jax.ShapeDtypeStruct((m, out_cols), BF16),
                   jax.ShapeDtypeStruct((m, LANES), F32)],
        scratch_shapes=[pltpu.VMEM((tm, d), BF16)],
        compiler_params=pltpu.CompilerParams(
            dimension_semantics=("arbitrary", "arbitrary"),
            vmem_limit_bytes=vmem + 4 * MIB),
        name="inproj",
    )(x2, g_norm, wt, wt, wt, rope)


def _forget_cumsum_kernel(fl_ref, b_ref, c_ref, *, heads):
    t = fl_ref[...] + b_ref[...]
    lf = jnp.minimum(t, 0.0) - jnp.log1p(jnp.exp(-jnp.abs(t)))
    seq = lf.shape[0]
    row = lax.broadcasted_iota(jnp.int32, lf.shape, 0)
    d = 1
    while d < seq:
        lf = lf + jnp.where(row >= d, pltpu.roll(lf, d, 0), 0.0)
        d *= 2
    c_ref[0] = lf.T[:heads, :]


def _forget_cumsum(fl, b_pad, *, batch, seq, heads):
    return pl.pallas_call(
        functools.partial(_forget_cumsum_kernel, heads=heads),
        grid=(batch,),
        in_specs=[pl.BlockSpec((seq, LANES), lambda b: (b, 0)),
                  pl.BlockSpec((1, LANES), lambda b: (0, 0))],
        out_specs=pl.BlockSpec((1, heads, seq), lambda b: (b, 0, 0)),
        out_shape=jax.ShapeDtypeStruct((batch, heads, seq), F32),
        compiler_params=pltpu.CompilerParams(dimension_semantics=("arbitrary",)),
        name="forget_cumsum",
    )(fl, b_pad)


ONES_ROWS = 16


def _silu(x):
    return x * (1.0 / (1.0 + jnp.exp(-x)))


HEADS_PER_STEP = 2


def _load_vt(v, vt_ref):
    hd = v.shape[1]
    vt_ref[:hd, :] = v.astype(F32).T.astype(vt_ref.dtype)
    vt_ref[hd:, :] = jnp.ones((ONES_ROWS, vt_ref.shape[1]), vt_ref.dtype)


def _pipelined_attention(n_blocks, n_kv, score_block, finish, s_bufs, vt_of, *, tk):
    hd = vt_of(0).shape[0] - ONES_ROWS

    def stage_scores(i):
        mrun = None
        for j in range(n_kv(i)):
            s = score_block(i, j)
            s_bufs[i % 2][j * tk:(j + 1) * tk, :] = s
            bm = jnp.max(s.reshape(tk // 8, 8, s.shape[1]), axis=0)
            mrun = bm if mrun is None else jnp.maximum(mrun, bm)
        return jnp.max(mrun, axis=0, keepdims=True)

    def probs(i, m):
        return jnp.exp2(s_bufs[i % 2][:n_kv(i) * tk, :] - m).astype(BF16)

    m_next = stage_scores(0)
    p_next = probs(0, m_next)
    if n_blocks > 1:
        m_next = stage_scores(1)
    for i in range(n_blocks):
        p, m = p_next, m_next
        if i + 2 < n_blocks:
            m_next = stage_scores(i + 2)
        if i + 1 < n_blocks:
            p_next = probs(i + 1, m)
        acc = jnp.dot(vt_of(i)[:, :n_kv(i) * tk], p, preferred_element_type=F32)
        finish(i, acc[:hd] * (1.0 / acc[hd:hd + 1]))


def _causal_mask(s, delta, off, tk):
    if off < tk - 1:
        s = jnp.where(delta <= off, s, -jnp.inf)
    return s


def _diff_attn_kernel(q_ref, k_ref, v_ref, g_ref, lq1_ref, lk1_ref, lq2_ref, lk2_ref,
                      gsub_ref, o_ref, s0_scr, s1_scr, vt_scr, *, tq, tk):
    seq = q_ref.shape[0]
    rows = 2 * tq
    lam = (jnp.exp(jnp.sum(lq1_ref[...] * lk1_ref[...], axis=-1, keepdims=True))
           - jnp.exp(jnp.sum(lq2_ref[...] * lk2_ref[...], axis=-1, keepdims=True))
           + LAMBDA_INIT)
    nq = seq // tq
    heads = q_ref.shape[1] // LANES
    for hh in range(heads):
        _load_vt(v_ref[:, hh * LANES:(hh + 1) * LANES], vt_scr.at[hh])
    map0 = _map0_lanes(lax.broadcasted_iota(jnp.int32, (tq, LANES), 1))
    delta = (lax.broadcasted_iota(jnp.int32, (tk, rows), 0)
             - (lax.broadcasted_iota(jnp.int32, (tk, rows), 1) & (tq - 1)))
    stacked = {}

    def stacked_q(b):
        if b not in stacked:
            hh, i = divmod(b, nq)
            q = q_ref[i * tq:(i + 1) * tq, hh * LANES:(hh + 1) * LANES]
            zero = jnp.zeros_like(q)
            stacked[b] = jnp.concatenate([jnp.where(map0, q, zero), jnp.where(map0, zero, q)],
                                         axis=0)
        return stacked[b]

    def score_block(b, j):
        hh, i = divmod(b, nq)
        raw = _dot_nt(k_ref[j * tk:(j + 1) * tk, hh * LANES:(hh + 1) * LANES], stacked_q(b))
        return _causal_mask(raw, delta, i * tq - j * tk, tk)

    def finish(b, ot):
        hh, i = divmod(b, nq)
        o = (ot[:, :tq] - lam * ot[:, tq:]).T
        ms = jnp.mean(o * o, axis=-1, keepdims=True)
        on = (o * lax.rsqrt(ms + EPS)) * gsub_ref[...] * (1.0 - LAMBDA_INIT)
        gate = g_ref[i * tq:(i + 1) * tq, hh * LANES:(hh + 1) * LANES].astype(F32)
        o_ref[i * tq:(i + 1) * tq, hh * LANES:(hh + 1) * LANES] = (
            on * _silu(gate)).astype(o_ref.dtype)

    _pipelined_attention(heads * nq, lambda b: (b % nq + 1) * tq // tk, score_block, finish,
                         (s0_scr, s1_scr), lambda b: vt_scr.at[b // nq], tk=tk)


def _fox_attn_kernel(q_ref, k_ref, v_ref, g_ref, c_ref, o_ref, s0_scr, s1_scr, vt_scr, *, tq, tk):
    seq = q_ref.shape[0]
    nq = seq // tq
    heads = q_ref.shape[1] // LANES
    crow, ccol = [], []
    for hh in range(heads):
        _load_vt(v_ref[:, hh * LANES:(hh + 1) * LANES], vt_scr.at[hh])
        row = c_ref[0, pl.ds(pl.program_id(1) * heads + hh, 1), :] * LOG2E
        crow.append(row)
        ccol.append(jnp.broadcast_to(row, (LANES, seq)).T)
    delta = (lax.broadcasted_iota(jnp.int32, (tk, tq), 0)
             - lax.broadcasted_iota(jnp.int32, (tk, tq), 1))

    def score_block(b, j):
        hh, i = divmod(b, nq)
        lanes = slice(hh * LANES, (hh + 1) * LANES)
        raw = _dot_nt(k_ref[j * tk:(j + 1) * tk, lanes], q_ref[i * tq:(i + 1) * tq, lanes])
        ck = ccol[hh][j * tk:(j + 1) * tk, :]
        parts = [raw[:, g * LANES:(g + 1) * LANES]
                 + (crow[hh][:, i * tq + g * LANES:i * tq + (g + 1) * LANES] - ck)
                 for g in range(tq // LANES)]
        return _causal_mask(jnp.concatenate(parts, axis=1), delta, i * tq - j * tk, tk)

    def finish(b, ot):
        hh, i = divmod(b, nq)
        lanes = slice(hh * LANES, (hh + 1) * LANES)
        gate = g_ref[i * tq:(i + 1) * tq, lanes].astype(F32)
        o_ref[i * tq:(i + 1) * tq, lanes] = (ot.T * _silu(gate)).astype(o_ref.dtype)

    _pipelined_attention(heads * nq, lambda b: (b % nq + 1) * tq // tk, score_block, finish,
                         (s0_scr, s1_scr), lambda b: vt_scr.at[b // nq], tk=tk)


def _head_spec(seq, col0):
    blk = HEADS_PER_STEP * LANES
    return pl.BlockSpec((seq, blk), lambda b, h: (b, col0 // blk + h))


def _diff_attn(z, lq1, lk1, lq2, lk2, g_subln, *, batch, seq, heads, col0, tq, tk):
    width = heads * DIFF_V_DIM
    vec = pl.BlockSpec((1, DIFF_QK_DIM), lambda b, h: (0, 0))
    return pl.pallas_call(
        functools.partial(_diff_attn_kernel, tq=tq, tk=tk),
        grid=(batch, heads // HEADS_PER_STEP),
        in_specs=[_head_spec(seq, col0), _head_spec(seq, col0 + width),
                  _head_spec(seq, col0 + 2 * width), _head_spec(seq, col0 + 3 * width),
                  vec, vec, vec, vec,
                  pl.BlockSpec((1, DIFF_V_DIM), lambda b, h: (0, 0))],
        out_specs=_head_spec(seq, 0),
        out_shape=jax.ShapeDtypeStruct((batch * seq, width), BF16),
        scratch_shapes=[pltpu.VMEM((seq, 2 * tq), F32), pltpu.VMEM((seq, 2 * tq), F32),
                        pltpu.VMEM((HEADS_PER_STEP, DIFF_V_DIM + ONES_ROWS, seq), BF16)],
        compiler_params=pltpu.CompilerParams(
            dimension_semantics=("arbitrary", "arbitrary"), vmem_limit_bytes=48 * MIB),
        name="diff_attn",
    )(z, z, z, z, lq1, lk1, lq2, lk2, g_subln)


def _fox_attn(z, c, *, batch, seq, heads, col0, tq, tk):
    width = heads * FOX_HEAD_DIM
    return pl.pallas_call(
        functools.partial(_fox_attn_kernel, tq=tq, tk=tk),
        grid=(batch, heads // HEADS_PER_STEP),
        in_specs=[_head_spec(seq, col0), _head_spec(seq, col0 + width),
                  _head_spec(seq, col0 + 2 * width), _head_spec(seq, col0 + 3 * width),
                  pl.BlockSpec((1, heads, seq), lambda b, h: (b, 0, 0))],
        out_specs=_head_spec(seq, 0),
        out_shape=jax.ShapeDtypeStruct((batch * seq, width), BF16),
        scratch_shapes=[pltpu.VMEM((seq, tq), F32), pltpu.VMEM((seq, tq), F32),
                        pltpu.VMEM((HEADS_PER_STEP, FOX_HEAD_DIM + ONES_ROWS, seq), BF16)],
        compiler_params=pltpu.CompilerParams(
            dimension_semantics=("arbitrary", "arbitrary"), vmem_limit_bytes=48 * MIB),
        name="fox_attn",
    )(z, z, z, z, c)


def _out_stage_kernel(oa_ref, ob_ref, ma_ref, mb_ref, wa_ref, wb_ref, wo_ref, x_ref, g_ref, o_ref,
                      *, chunk):
    d = wo_ref.shape[0]
    y = x_ref[...]
    for c in range(d // chunk):
        cols = slice(c * chunk, (c + 1) * chunk)
        ya = jnp.dot(oa_ref[...], wa_ref[:, cols], preferred_element_type=F32)
        yb = jnp.dot(ob_ref[...], wb_ref[:, cols], preferred_element_type=F32)
        mrg = (jax.nn.sigmoid(ma_ref[:, cols].astype(F32)) * ya
               + jax.nn.sigmoid(mb_ref[:, cols].astype(F32)) * yb)
        y = y + jnp.dot(mrg.astype(BF16), wo_ref[cols, :], preferred_element_type=F32)
    ms = jnp.mean(y * y, axis=-1, keepdims=True)
    o_ref[...] = (y * lax.rsqrt(ms + EPS)) * g_ref[...]


def _out_stage(oa, ob, z, wa, wb, wo, x2, g_final, *, ma_col0, mb_col0, tm):
    m, d = x2.shape
    wdt = oa.shape[1]

    def resident(shape):
        return pl.BlockSpec(shape, lambda i: (0, 0), pipeline_mode=pl.Buffered(1))

    vmem = (2 * (2 * tm * wdt * 2 + 2 * tm * d * 2 + 2 * tm * d * 4)
            + (2 * wdt * d + d * d) * 2 + 2 * tm * d * 4)
    return pl.pallas_call(
        functools.partial(_out_stage_kernel, chunk=512),
        grid=(m // tm,),
        in_specs=[pl.BlockSpec((tm, wdt), lambda i: (i, 0)),
                  pl.BlockSpec((tm, wdt), lambda i: (i, 0)),
                  pl.BlockSpec((tm, d), lambda i: (i, ma_col0 // d)),
                  pl.BlockSpec((tm, d), lambda i: (i, mb_col0 // d)),
                  resident((wdt, d)), resident((wdt, d)), resident((d, d)),
                  pl.BlockSpec((tm, d), lambda i: (i, 0)),
                  resident((1, d))],
        out_specs=pl.BlockSpec((tm, d), lambda i: (i, 0)),
        out_shape=jax.ShapeDtypeStruct((m, d), F32),
        compiler_params=pltpu.CompilerParams(
            dimension_semantics=("arbitrary",), vmem_limit_bytes=vmem),
        name="out_stage",
    )(oa, ob, z, z, wa, wb, wo, x2, g_final)


def kernel(x, g_norm, w_in, lambda_q1, lambda_k1, lambda_q2, lambda_k2, g_subln, b_forget,
           w_proj_a, w_proj_b, w_out, g_final):
    batch, seq, d = x.shape
    diff_width = w_proj_a.shape[1]
    fox_width = w_proj_b.shape[1]
    diff_heads = diff_width // DIFF_V_DIM
    fox_heads = fox_width // FOX_HEAD_DIM
    main_cols = 4 * diff_width + 4 * fox_width
    assert w_in.shape[0] == 1 and w_in.shape[2] == main_cols + fox_heads + 2 * d

    wa, wb, wo = _cast3_bf16(w_proj_a[0], w_proj_b[0], w_out[0], steps=4)
    b_pad = jnp.pad(b_forget, ((0, 0), (0, LANES - fox_heads)))

    x2 = x.reshape(batch * seq, d)
    z, fl = _inproj(x2, g_norm, jnp.swapaxes(w_in[0], 0, 1), _rope_tables(seq),
                    seq=seq, tm=1024, tn=1024, qk_cols=diff_width, fox_q_col0=4 * diff_width,
                    main_cols=main_cols, shift=fox_heads, out_cols=main_cols + 2 * d)
    c = _forget_cumsum(fl, b_pad, batch=batch, seq=seq, heads=fox_heads)
    oa = _diff_attn(z, lambda_q1, lambda_k1, lambda_q2, lambda_k2, g_subln,
                    batch=batch, seq=seq, heads=diff_heads, col0=0, tq=256, tk=256)
    ob = _fox_attn(z, c, batch=batch, seq=seq, heads=fox_heads, col0=4 * diff_width,
                   tq=512, tk=256)
    out = _out_stage(oa, ob, z, wa, wb, wo, x2, g_final.reshape(1, d),
                     ma_col0=main_cols, mb_col0=main_cols + d, tm=512)
    return out.reshape(batch, seq, d)
```
